```python
import math
import jax, jax.numpy as jnp
from jax import lax
import numpy as np

D_MODEL = 1024
BATCH = 2
SEQ = 16384
DEPTH = 2

CHUNK = 64
QBLOCK = 128
MAX_STREAM_OFFSET = 4096

D_SSD = D_MODEL
SSD_HEADDIM = 64
SSD_HEADS = D_SSD // SSD_HEADDIM
SSD_GROUPS = 4
D_STATE = 128
SSD_CONV = 4
SSD_CHUNK = CHUNK

D_CONV = D_MODEL
CONF_KERNEL = 31

DIFF_HEADS = 8
DIFF_DK = 64
DIFF_DV = 2 * DIFF_DK
D_ATTN = DIFF_HEADS * DIFF_DV
ROT_DIM = DIFF_DK // 4
ROPE_THETA = 500000.0

EPS = 1e-6
LN_EPS = 1e-5

N_EVEN = (DEPTH + 1) // 2
N_ODD = DEPTH // 2

XBC_DIM = D_SSD + 2 * SSD_GROUPS * D_STATE
EVEN_SPLITS = [D_SSD, XBC_DIM, SSD_HEADS, D_CONV, D_CONV, D_CONV]
EVEN_IN = sum(EVEN_SPLITS)
EVEN_MIX = D_SSD + D_CONV
ODD_IN = 4 * D_ATTN

kernel_name = "hybrid_ssd_conformer_diffattn_trunk"


def rmsnorm(x, w, eps=EPS):
    xf = x.astype(jnp.float32)
    y = xf * lax.rsqrt(jnp.mean(xf * xf, axis=-1, keepdims=True) + eps)
    return (y * w.astype(jnp.float32)).astype(x.dtype)


def layernorm(x, w, b, eps=LN_EPS):
    xf = x.astype(jnp.float32)
    mu = jnp.mean(xf, axis=-1, keepdims=True)
    var = jnp.mean(jnp.square(xf - mu), axis=-1, keepdims=True)
    y = (xf - mu) * lax.rsqrt(var + eps)
    return (y * w.astype(jnp.float32) + b.astype(jnp.float32)).astype(x.dtype)


def causal_dwconv(x, w, b):
    k = w.shape[0]
    out = lax.conv_general_dilated(
        x, w[:, None, :].astype(x.dtype), window_strides=(1,), padding=[(k - 1, 0)],
        dimension_numbers=("NWC", "WIO", "NWC"), feature_group_count=x.shape[-1])
    return out + b.astype(x.dtype)


def ssd_chunked(x, dt, a, bm, cm, d_skip):
    bsz, s, h, p = x.shape
    g, n = bm.shape[2], bm.shape[3]
    e = h // g
    l = SSD_CHUNK
    nc = s // l
    xd = (x * dt[..., None]).reshape(bsz, nc, l, g, e, p)
    da = (dt * a).reshape(bsz, nc, l, g, e).transpose(0, 3, 4, 1, 2)
    cs = jnp.cumsum(da, axis=-1)
    bm = bm.reshape(bsz, nc, l, g, n)
    cm = cm.reshape(bsz, nc, l, g, n)
    tril = jnp.tril(jnp.ones((l, l), dtype=bool))
    seg = cs[..., :, None] - cs[..., None, :]
    decay = jnp.exp(jnp.where(tril, seg, -jnp.inf))
    cb = jnp.einsum("bclgn,bcsgn->bcgls", cm, bm)
    y_diag = jnp.einsum("bcgls,bgecls,bcsgep->bclgep", cb, decay, xd)
    decay_states = jnp.exp(cs[..., -1:] - cs)
    states = jnp.einsum("bclgn,bgecl,bclgep->bcgepn", bm, decay_states, xd)
    chunk_decay = jnp.exp(cs[..., -1])

    def step(hstate, inp):
        st, dec = inp
        return hstate * dec[..., None, None] + st, hstate

    h0 = jnp.zeros((bsz, g, e, p, n), jnp.float32)
    _, prev = lax.scan(step, h0, (jnp.moveaxis(states, 1, 0), jnp.moveaxis(chunk_decay, -1, 0)))
    prev = jnp.moveaxis(prev, 0, 1)
    y_off = jnp.einsum("bclgn,bcgepn,bgecl->bclgep", cm, prev, jnp.exp(cs))
    y = (y_diag + y_off).reshape(bsz, s, h, p)
    return y + x * d_skip[:, None]


def partial_rotary(t, pos):
    half = ROT_DIM // 2
    inv = ROPE_THETA ** (-2.0 * jnp.arange(half, dtype=jnp.float32) / ROT_DIM)
    ang = pos.astype(jnp.float32)[..., None] * inv
    cos = jnp.cos(ang)[:, :, None, None, :]
    sin = jnp.sin(ang)[:, :, None, None, :]
    tf = t.astype(jnp.float32)
    t1, t2 = tf[..., :half], tf[..., half:ROT_DIM]
    out = jnp.concatenate([t1 * cos - t2 * sin, t2 * cos + t1 * sin, tf[..., ROT_DIM:]], axis=-1)
    return out.astype(t.dtype)


def diff_attention(q, k, v, lam):
    bsz, s, h, _, dk = q.shape
    nb = s // QBLOCK
    qb = q.reshape(bsz, nb, QBLOCK, h, 2, dk).swapaxes(0, 1)
    kf = k.astype(jnp.float32)
    key_chunk = jnp.arange(s) // CHUNK
    scale = dk ** -0.5

    def one_block(args):
        qblk, bi = args
        sc = jnp.einsum("bqhmd,bkhmd->bhmqk", qblk.astype(jnp.float32), kf) * scale
        q_chunk = (bi * QBLOCK + jnp.arange(QBLOCK)) // CHUNK
        allowed = key_chunk[None, :] <= q_chunk[:, None]
        sc = jnp.where(allowed, sc, -jnp.inf)
        pr = jax.nn.softmax(sc, axis=-1)
        wts = pr[:, :, 0] - lam * pr[:, :, 1]
        return jnp.einsum("bhqk,bkhe->bqhe", wts.astype(v.dtype), v)

    o = lax.map(one_block, (qb, jnp.arange(nb)))
    return o.swapaxes(0, 1).reshape(bsz, s, h, v.shape[-1])


def even_layer(x, norm_w, w_in, conv_w, conv_b, dt_bias, a_log, d_skip, ssd_norm_w,
               dw_w, dw_b, ln_w, ln_b, w_out):
    bsz, s, _ = x.shape
    hn = rmsnorm(x, norm_w)
    proj = hn @ w_in
    idx = list(np.cumsum(EVEN_SPLITS)[:-1])
    z_a, xbc, dt_raw, glu_v, glu_g, z_b = jnp.split(proj, idx, axis=-1)
    xbc = jax.nn.silu(causal_dwconv(xbc, conv_w, conv_b))
    xs, bs, cs = jnp.split(xbc, [D_SSD, D_SSD + SSD_GROUPS * D_STATE], axis=-1)
    xs = xs.reshape(bsz, s, SSD_HEADS, SSD_HEADDIM).astype(jnp.float32)
    bs = bs.reshape(bsz, s, SSD_GROUPS, D_STATE).astype(jnp.float32)
    cs = cs.reshape(bsz, s, SSD_GROUPS, D_STATE).astype(jnp.float32)
    dt = jax.nn.softplus(dt_raw.astype(jnp.float32) + dt_bias.astype(jnp.float32))
    a = -jnp.exp(a_log.astype(jnp.float32))
    y = ssd_chunked(xs, dt, a, bs, cs, d_skip.astype(jnp.float32)).reshape(bsz, s, D_SSD)
    y = y * jax.nn.silu(z_a.astype(jnp.float32))
    y_a = rmsnorm(y.reshape(bsz, s, SSD_GROUPS, D_SSD // SSD_GROUPS),
                  ssd_norm_w.reshape(SSD_GROUPS, D_SSD // SSD_GROUPS)).reshape(bsz, s, D_SSD)
    y_a = y_a.astype(x.dtype)
    u = glu_v * jax.nn.sigmoid(glu_g)
    u = causal_dwconv(u, dw_w, dw_b)
    u = jax.nn.silu(layernorm(u, ln_w, ln_b))
    y_b = (u * jax.nn.silu(z_b)).astype(x.dtype)
    return x + jnp.concatenate([y_a, y_b], axis=-1) @ w_out


def odd_layer(x, positions, layer_idx, norm_w, w_in, q_norm_w, k_norm_w,
              lq1, lk1, lq2, lk2, subln_w, w_out):
    bsz, s, _ = x.shape
    lam_init = 0.8 - 0.6 * math.exp(-0.3 * layer_idx)
    hn = rmsnorm(x, norm_w)
    proj = hn @ w_in
    q, k, v, gate = jnp.split(proj, 4, axis=-1)
    q = rmsnorm(q.reshape(bsz, s, DIFF_HEADS, 2, DIFF_DK), q_norm_w)
    k = rmsnorm(k.reshape(bsz, s, DIFF_HEADS, 2, DIFF_DK), k_norm_w)
    q = partial_rotary(q, positions)
    k = partial_rotary(k, positions)
    v = v.reshape(bsz, s, DIFF_HEADS, DIFF_DV)
    f32 = jnp.float32
    lam = (jnp.exp(jnp.sum(lq1.astype(f32) * lk1.astype(f32)))
           - jnp.exp(jnp.sum(lq2.astype(f32) * lk2.astype(f32))) + lam_init)
    o = diff_attention(q, k, v, lam)
    o = rmsnorm(o, subln_w) * (1.0 - lam_init)
    o = o.reshape(bsz, s, D_ATTN) * jax.nn.silu(gate)
    return x + o.astype(x.dtype) @ w_out


def setup_inputs(seed: int = 0) -> dict:
    key = jax.random.key(seed)
    ks = iter(jax.random.split(key, 40))

    def nrm(shape, scale):
        return jax.random.normal(next(ks), shape, jnp.float32) * scale

    x = nrm((BATCH, SEQ, D_MODEL), 1.0)
    offset = jax.random.randint(next(ks), (BATCH, 1), 0, MAX_STREAM_OFFSET, dtype=jnp.int32)
    positions = (offset + jnp.arange(SEQ, dtype=jnp.int32)[None, :]).astype(jnp.int32)

    ne, no = N_EVEN, N_ODD
    u = jax.random.uniform(next(ks), (ne, SSD_HEADS), jnp.float32)
    dt0 = jnp.exp(u * (math.log(0.1) - math.log(0.001)) + math.log(0.001))
    a_dt_bias = dt0 + jnp.log(-jnp.expm1(-dt0))
    a_a_log = jnp.log(jax.random.uniform(next(ks), (ne, SSD_HEADS), jnp.float32, 1.0, 16.0))
    return {
        "x": x,
        "positions": positions,
        "a_norm_w": 1.0 + nrm((ne, D_MODEL), 0.05),
        "a_w_in": nrm((ne, D_MODEL, EVEN_IN), D_MODEL ** -0.5),
        "a_conv_w": nrm((ne, SSD_CONV, XBC_DIM), SSD_CONV ** -0.5),
        "a_conv_b": nrm((ne, XBC_DIM), 0.02),
        "a_dt_bias": a_dt_bias,
        "a_a_log": a_a_log,
        "a_d_skip": 1.0 + nrm((ne, SSD_HEADS), 0.05),
        "a_ssd_norm_w": 1.0 + nrm((ne, D_SSD), 0.05),
        "a_dw_w": nrm((ne, CONF_KERNEL, D_CONV), CONF_KERNEL ** -0.5),
        "a_dw_b": nrm((ne, D_CONV), 0.02),
        "a_ln_w": 1.0 + nrm((ne, D_CONV), 0.05),
        "a_ln_b": nrm((ne, D_CONV), 0.02),
        "a_w_out": nrm((ne, EVEN_MIX, D_MODEL), EVEN_MIX ** -0.5),
        "c_norm_w": 1.0 + nrm((no, D_MODEL), 0.05),
        "c_w_in": nrm((no, D_MODEL, ODD_IN), D_MODEL ** -0.5),
        "c_q_norm_w": 1.0 + nrm((no, DIFF_DK), 0.05),
        "c_k_norm_w": 1.0 + nrm((no, DIFF_DK), 0.05),
        "c_lq1": nrm((no, DIFF_DK), 0.1),
        "c_lk1": nrm((no, DIFF_DK), 0.1),
        "c_lq2": nrm((no, DIFF_DK), 0.1),
        "c_lk2": nrm((no, DIFF_DK), 0.1),
        "c_subln_w": 1.0 + nrm((no, DIFF_DV), 0.05),
        "c_w_out": nrm((no, D_ATTN, D_MODEL), D_ATTN ** -0.5),
    }


def reference(x, positions, a_norm_w, a_w_in, a_conv_w, a_conv_b, a_dt_bias, a_a_log,
              a_d_skip, a_ssd_norm_w, a_dw_w, a_dw_b, a_ln_w, a_ln_b, a_w_out,
              c_norm_w, c_w_in, c_q_norm_w, c_k_norm_w, c_lq1, c_lk1, c_lq2, c_lk2,
              c_subln_w, c_w_out):
    for i in range(DEPTH):
        if i % 2 == 0:
            j = i // 2
            x = even_layer(x, a_norm_w[j], a_w_in[j], a_conv_w[j], a_conv_b[j], a_dt_bias[j],
                           a_a_log[j], a_d_skip[j], a_ssd_norm_w[j], a_dw_w[j], a_dw_b[j],
                           a_ln_w[j], a_ln_b[j], a_w_out[j])
        else:
            j = i // 2
            x = odd_layer(x, positions, i, c_norm_w[j], c_w_in[j], c_q_norm_w[j], c_k_norm_w[j],
                          c_lq1[j], c_lk1[j], c_lq2[j], c_lk2[j], c_subln_w[j], c_w_out[j])
    return x
```

```python
import functools
import math

import jax
import jax.numpy as jnp
import numpy as np
from jax import lax
from jax.experimental import pallas as pl
from jax.experimental.pallas import tpu as pltpu

F32 = jnp.float32
BF16 = jnp.bfloat16

D_MODEL = 1024
CHUNK = 64
SSD_HEADS = 16
SSD_HEADDIM = 64
SSD_GROUPS = 4
HEADS_PER_GROUP = SSD_HEADS // SSD_GROUPS
D_STATE = 128
SSD_CONV = 4
CONF_KERNEL = 31
DIFF_HEADS = 8
DIFF_DK = 64
DIFF_DV = 128
ROT_DIM = 16
ROPE_THETA = 500000.0
EPS = 1e-6
LN_EPS = 1e-5
ODD_LAYER_IDX = 1
LAM_INIT = 0.8 - 0.6 * math.exp(-0.3 * ODD_LAYER_IDX)

LANES = 128
SUBLANES = 8
VMEM_LIMIT_BYTES = 56 * 1024 * 1024

PROJ_ROWS = 512
MIX_ROWS = 256
PAIR = 2 * CHUNK
XCARRY = 8
UCARRY = 32
ATT_TQ = 256
ATT_TK = 256
NEG = -1e30


def _dot(a, b):
    return jnp.dot(a, b, preferred_element_type=F32)


def _dot_nt(a, b):
    return lax.dot_general(a, b, (((1,), (1,)), ((), ())), preferred_element_type=F32)


def _split_bf16(x, parts):
    out = []
    r = x
    for i in range(parts):
        h = r.astype(BF16)
        out.append(h)
        if i + 1 < parts:
            r = r - h.astype(F32)
    return out


def _expand(x, e_ref, parts=3):
    e = e_ref[...]
    acc = None
    for h in _split_bf16(x, parts):
        d = _dot(h, e)
        acc = d if acc is None else acc + d
    return acc


def _left_apply(m_ref, x, parts=3):
    m = m_ref[...]
    acc = None
    for h in _split_bf16(x, parts):
        d = _dot(m, h)
        acc = d if acc is None else acc + d
    return acc


def _rms_rows(x, w):
    ms = jnp.mean(x * x, axis=-1, keepdims=True)
    return x * lax.rsqrt(ms + EPS) * w


def _even_inproj_kernel(x_ref, nw_ref, w_ref, wdt_ref, proj_ref, dt_ref):
    hn = _rms_rows(x_ref[...], nw_ref[...]).astype(BF16)
    n_out = proj_ref.shape[1]
    for c in range(0, n_out, D_MODEL):
        proj_ref[:, c:c + D_MODEL] = _dot(hn, w_ref[:, c:c + D_MODEL]).astype(BF16)
    dt_ref[...] = _dot(hn, wdt_ref[...])


def _even_inproj(x2, nw, w, wdt):
    m = x2.shape[0]
    n_out = w.shape[1]
    tm = PROJ_ROWS
    return pl.pallas_call(
        _even_inproj_kernel,
        grid=(m // tm,),
        in_specs=[
            pl.BlockSpec((tm, D_MODEL), lambda i: (i, 0)),
            pl.BlockSpec((1, D_MODEL), lambda i: (0, 0)),
            pl.BlockSpec((D_MODEL, n_out), lambda i: (0, 0), pipeline_mode=pl.Buffered(1)),
            pl.BlockSpec((D_MODEL, LANES), lambda i: (0, 0), pipeline_mode=pl.Buffered(1)),
        ],
        out_specs=[
            pl.BlockSpec((tm, n_out), lambda i: (i, 0)),
            pl.BlockSpec((tm, LANES), lambda i: (i, 0)),
        ],
        out_shape=[
            jax.ShapeDtypeStruct((m, n_out), BF16),
            jax.ShapeDtypeStruct((m, LANES), F32),
        ],
        compiler_params=pltpu.CompilerParams(
            dimension_semantics=("arbitrary",), vmem_limit_bytes=VMEM_LIMIT_BYTES),
        name="even_inproj",
    )(x2, nw, w, wdt)


def _even_mix_kernel(xbc_ref, za_ref, gv_ref, gg_ref, zb_ref, dt_ref,
                     cw_ref, cb_ref, dtb_ref, alog_ref, dskip_ref, nrm_ref,
                     dww_ref, dwb_ref, lnw_ref, lnb_ref,
                     e64_ref, e128_ref, tri_ref, blk_ref,
                     out_ref,
                     xbuf, ubuf, hs, ybuf):
    t = pl.program_id(1)
    rows = za_ref.shape[0]
    d_ssd = SSD_HEADS * SSD_HEADDIM
    gw = HEADS_PER_GROUP * SSD_HEADDIM

    @pl.when(t == 0)
    def _():
        xbuf[0:XCARRY, :] = jnp.zeros((XCARRY, xbuf.shape[1]), F32)
        ubuf[0:UCARRY, :] = jnp.zeros((UCARRY, ubuf.shape[1]), F32)
        hs[...] = jnp.zeros(hs.shape, F32)

    xbuf[XCARRY:XCARRY + rows, :] = xbc_ref[...].astype(F32)
    acc = cb_ref[...]
    for k in range(SSD_CONV):
        off = XCARRY - (SSD_CONV - 1) + k
        acc = acc + cw_ref[k:k + 1, :] * xbuf[off:off + rows, :]
    xbuf[0:XCARRY, :] = xbuf[rows:rows + XCARRY, :]
    xc = acc * jax.nn.sigmoid(acc)
    xs = xc[:, :d_ssd]
    bm = xc[:, d_ssd:d_ssd + SSD_GROUPS * D_STATE]
    cm = xc[:, d_ssd + SSD_GROUPS * D_STATE:]

    dtr = dt_ref[...] + dtb_ref[...]
    dt = jnp.maximum(dtr, 0.0) + jnp.log1p(jnp.exp(-jnp.abs(dtr)))
    da = dt * (-jnp.exp(alog_ref[...]))
    cs = _left_apply(tri_ref, da)
    tot = _left_apply(blk_ref, da)
    dt_x = _expand(dt, e64_ref)
    ecs_x = _expand(jnp.exp(cs), e64_ref)
    edst_x = _expand(jnp.exp(tot - cs), e64_ref)
    ecd = jnp.exp(tot)
    cs_col = _expand(cs, e128_ref)
    cs_t = cs.T

    xd = xs * dt_x
    xdd = xd * edst_x

    li = lax.broadcasted_iota(jnp.int32, (PAIR, PAIR), 0)
    si = lax.broadcasted_iota(jnp.int32, (PAIR, PAIR), 1)
    intra = (si <= li) & ((si // CHUNK) == (li // CHUNK))
    lane_g = lax.broadcasted_iota(jnp.int32, (PAIR, gw), 1)
    row_g = lax.broadcasted_iota(jnp.int32, (PAIR, gw), 0)

    for pr in range(rows // PAIR):
        r0 = pr * PAIR
        for g in range(SSD_GROUPS):
            c_g = cm[r0:r0 + PAIR, g * D_STATE:(g + 1) * D_STATE].astype(BF16)
            b_g = bm[r0:r0 + PAIR, g * D_STATE:(g + 1) * D_STATE]
            cb = _dot_nt(c_g, b_g.astype(BF16))
            b_gt = b_g.T.astype(BF16)
            xd_g = xd[r0:r0 + PAIR, g * gw:(g + 1) * gw]
            xdd_g = xdd[r0:r0 + PAIR, g * gw:(g + 1) * gw]
            m_parts = []
            w_parts = []
            for e in range(HEADS_PER_GROUP):
                h = g * HEADS_PER_GROUP + e
                seg = cs_col[r0:r0 + PAIR, h * LANES:(h + 1) * LANES] - cs_t[h:h + 1, r0:r0 + PAIR]
                decay = jnp.exp(jnp.where(intra, seg, NEG))
                m_parts.append((cb * decay).astype(BF16))
                in_head = (lane_g // SSD_HEADDIM) == e
                w_parts.append(jnp.where(in_head, xd_g, 0.0).astype(BF16))
            y_diag = _dot(jnp.concatenate(m_parts, axis=1), jnp.concatenate(w_parts, axis=0))
            for j in range(PAIR // CHUNK):
                rj = r0 + j * CHUNK
                h_prev = hs[g]
                y_off = _dot(c_g[j * CHUNK:(j + 1) * CHUNK], h_prev.astype(BF16))
                y_off = y_off * ecs_x[rj:rj + CHUNK, g * gw:(g + 1) * gw]
                ybuf[rj:rj + CHUNK, g * gw:(g + 1) * gw] = y_diag[j * CHUNK:(j + 1) * CHUNK] + y_off
                in_chunk = (row_g // CHUNK) == j
                st = _dot(b_gt, jnp.where(in_chunk, xdd_g, 0.0).astype(BF16))
                ecd_x = _expand(ecd[rj:rj + SUBLANES], e64_ref)[0:1, g * gw:(g + 1) * gw]
                hs[g] = h_prev * ecd_x + st

    y = ybuf[...] + xs * dskip_ref[...]
    za = za_ref[...].astype(F32)
    y = y * (za * jax.nn.sigmoid(za))
    for g in range(SSD_GROUPS):
        yg = y[:, g * gw:(g + 1) * gw]
        out_ref[:, g * gw:(g + 1) * gw] = _rms_rows(yg, nrm_ref[:, g * gw:(g + 1) * gw]).astype(BF16)

    gg = gg_ref[...].astype(F32)
    ubuf[UCARRY:UCARRY + rows, :] = gv_ref[...].astype(F32) * jax.nn.sigmoid(gg)
    acc_u = dwb_ref[...]
    base = UCARRY - (CONF_KERNEL - 1)
    for sh in range(SUBLANES):
        taps = [k for k in range(CONF_KERNEL) if (base + k) % SUBLANES == sh]
        if not taps:
            continue
        span = (base + taps[-1] - sh) + rows
        shifted = ubuf[sh:sh + span, :]
        for k in taps:
            a0 = base + k - sh
            acc_u = acc_u + dww_ref[k:k + 1, :] * shifted[a0:a0 + rows, :]
    ubuf[0:UCARRY, :] = ubuf[rows:rows + UCARRY, :]
    mu = jnp.mean(acc_u, axis=-1, keepdims=True)
    cen = acc_u - mu
    var = jnp.mean(cen * cen, axis=-1, keepdims=True)
    un = cen * lax.rsqrt(var + LN_EPS) * lnw_ref[...] + lnb_ref[...]
    un = un * jax.nn.sigmoid(un)
    zb = zb_ref[...].astype(F32)
    out_ref[:, d_ssd:] = (un * (zb * jax.nn.sigmoid(zb))).astype(BF16)


def _even_mix(proj, dt_raw, p, bsz, seq):
    rows = MIX_ROWS
    nt = seq // rows
    d_ssd = SSD_HEADS * SSD_HEADDIM
    xbc_dim = d_ssd + 2 * SSD_GROUPS * D_STATE
    m = bsz * seq

    def rowblk(col):
        return lambda b, t: (b * nt + t, col)

    def const(shape):
        return pl.BlockSpec(shape, lambda b, t: (0, 0))

    in_specs = [
        pl.BlockSpec((rows, xbc_dim), rowblk(0)),
        pl.BlockSpec((rows, D_MODEL), rowblk(2)),
        pl.BlockSpec((rows, D_MODEL), rowblk(3)),
        pl.BlockSpec((rows, D_MODEL), rowblk(4)),
        pl.BlockSpec((rows, D_MODEL), rowblk(5)),
        pl.BlockSpec((rows, LANES), rowblk(0)),
        const((SSD_CONV, xbc_dim)), const((1, xbc_dim)),
        const((1, LANES)), const((1, LANES)),
        const((1, d_ssd)), const((1, d_ssd)),
        const((CONF_KERNEL, D_MODEL)), const((1, D_MODEL)), const((1, D_MODEL)), const((1, D_MODEL)),
        const((LANES, d_ssd)), const((LANES, SSD_HEADS * LANES)),
        const((rows, rows)), const((rows, rows)),
    ]
    return pl.pallas_call(
        _even_mix_kernel,
        grid=(bsz, nt),
        in_specs=in_specs,
        out_specs=pl.BlockSpec((rows, 2 * D_MODEL), rowblk(0)),
        out_shape=jax.ShapeDtypeStruct((m, 2 * D_MODEL), BF16),
        scratch_shapes=[
            pltpu.VMEM((XCARRY + rows, xbc_dim), F32),
            pltpu.VMEM((UCARRY + rows, D_MODEL), F32),
            pltpu.VMEM((SSD_GROUPS, D_STATE, HEADS_PER_GROUP * SSD_HEADDIM), F32),
            pltpu.VMEM((rows, d_ssd), F32),
        ],
        compiler_params=pltpu.CompilerParams(
            dimension_semantics=("arbitrary", "arbitrary"), vmem_limit_bytes=VMEM_LIMIT_BYTES),
        name="even_mix",
    )(proj, proj, proj, proj, proj, dt_raw,
      p["conv_w"], p["conv_b"], p["dt_bias"], p["a_log"], p["d_skip"], p["ssd_norm_w"],
      p["dw_w"], p["dw_b"], p["ln_w"], p["ln_b"],
      p["e64"], p["e128"], p["tri"], p["blk"])


def _outproj_kernel(y_ref, w_ref, x_ref, o_ref):
    o_ref[...] = x_ref[...] + _dot(y_ref[...], w_ref[...])


def _outproj(y, w, x2, name):
    m, kdim = y.shape
    tm = PROJ_ROWS
    return pl.pallas_call(
        _outproj_kernel,
        grid=(m // tm,),
        in_specs=[
            pl.BlockSpec((tm, kdim), lambda i: (i, 0)),
            pl.BlockSpec((kdim, D_MODEL), lambda i: (0, 0), pipeline_mode=pl.Buffered(1)),
            pl.BlockSpec((tm, D_MODEL), lambda i: (i, 0)),
        ],
        out_specs=pl.BlockSpec((tm, D_MODEL), lambda i: (i, 0)),
        out_shape=jax.ShapeDtypeStruct((m, D_MODEL), F32),
        compiler_params=pltpu.CompilerParams(
            dimension_semantics=("arbitrary",), vmem_limit_bytes=VMEM_LIMIT_BYTES),
        name=name,
    )(y, w, x2)


def _odd_inproj_kernel(x_ref, pos_ref, nw_ref, w_ref, qnw_ref, knw_ref, invf_ref, sgn_ref, gmean_ref,
                       q_ref, k_ref, v_ref, g_ref):
    hn = _rms_rows(x_ref[...], nw_ref[...]).astype(BF16)
    d = D_MODEL
    ang = pos_ref[...].astype(F32) * invf_ref[...]
    reps = d // LANES
    cos = jnp.concatenate([jnp.cos(ang)] * reps, axis=1)
    sin = jnp.concatenate([jnp.sin(ang) * sgn_ref[...]] * reps, axis=1)
    lane = lax.broadcasted_iota(jnp.int32, (1, d), 1) % DIFF_DK
    first_half = lane < (ROT_DIM // 2)
    gblk = gmean_ref.shape[0]

    def norm_rot(z, w_row, scale):
        sq = z * z
        parts = []
        for c in range(0, d, gblk):
            parts.append(_expand(sq[:, c:c + gblk], gmean_ref, parts=2))
        ms = jnp.concatenate(parts, axis=1)
        zn = z * lax.rsqrt(ms + EPS) * w_row
        swapped = jnp.where(first_half, pltpu.roll(zn, d - ROT_DIM // 2, 1), pltpu.roll(zn, ROT_DIM // 2, 1))
        out = zn * cos + swapped * sin
        if scale != 1.0:
            out = out * scale
        return out.astype(BF16)

    q = _dot(hn, w_ref[:, 0:d])
    q_ref[...] = norm_rot(q, qnw_ref[...], math.log2(math.e) * DIFF_DK ** -0.5)
    k = _dot(hn, w_ref[:, d:2 * d])
    k_ref[...] = norm_rot(k, knw_ref[...], 1.0)
    v_ref[...] = _dot(hn, w_ref[:, 2 * d:3 * d]).astype(BF16)
    g_ref[...] = _dot(hn, w_ref[:, 3 * d:4 * d]).astype(BF16)


def _odd_inproj(x2, pos2, p):
    m = x2.shape[0]
    tm = PROJ_ROWS // 2
    row = lambda i: (i, 0)
    const = lambda i: (0, 0)
    gblk = p["gmean"].shape[0]
    out_sds = jax.ShapeDtypeStruct((m, D_MODEL), BF16)
    return pl.pallas_call(
        _odd_inproj_kernel,
        grid=(m // tm,),
        in_specs=[
            pl.BlockSpec((tm, D_MODEL), row),
            pl.BlockSpec((tm, 1), row),
            pl.BlockSpec((1, D_MODEL), const),
            pl.BlockSpec((D_MODEL, 4 * D_MODEL), const, pipeline_mode=pl.Buffered(1)),
            pl.BlockSpec((1, D_MODEL), const),
            pl.BlockSpec((1, D_MODEL), const),
            pl.BlockSpec((1, LANES), const),
            pl.BlockSpec((1, LANES), const),
            pl.BlockSpec((gblk, gblk), const),
        ],
        out_specs=[pl.BlockSpec((tm, D_MODEL), row)] * 4,
        out_shape=[out_sds] * 4,
        compiler_params=pltpu.CompilerParams(
            dimension_semantics=("arbitrary",), vmem_limit_bytes=VMEM_LIMIT_BYTES),
        name="odd_inproj",
    )(x2, pos2, p["norm_w"], p["w_in"], p["q_norm_w"], p["k_norm_w"], p["invf"], p["sgn"], p["gmean"])


def _attn_kernel(q_ref, k_ref, v_ref, gate_ref, lq1_ref, lk1_ref, lq2_ref, lk2_ref, sub_ref,
                 o_ref, m_sc, l_sc, acc_sc):
    qi = pl.program_id(2)
    tq = q_ref.shape[1]
    tk = ATT_TK
    q = q_ref[0]
    lane = lax.broadcasted_iota(jnp.int32, q.shape, 1)
    zero = jnp.zeros_like(q)
    q_maps = (jnp.where(lane < DIFF_DK, q, zero), jnp.where(lane >= DIFF_DK, q, zero))

    m_sc[...] = jnp.full(m_sc.shape, NEG, F32)
    l_sc[...] = jnp.zeros(l_sc.shape, F32)
    acc_sc[...] = jnp.zeros(acc_sc.shape, F32)

    def step(kj, vj, allowed):
        for mp in range(2):
            s = _dot_nt(q_maps[mp], kj)
            if allowed is not None:
                s = jnp.where(allowed, s, NEG)
            m_prev = m_sc[mp]
            m_new = jnp.maximum(m_prev, jnp.max(s, axis=1, keepdims=True))
            p = jnp.exp2(s - jnp.concatenate([m_new] * (tk // LANES), axis=1))
            alpha = jnp.exp2(m_prev - m_new)
            l_sc[mp] = alpha * l_sc[mp] + jnp.sum(p, axis=1, keepdims=True)
            acc_sc[mp] = alpha * acc_sc[mp] + _dot(p.astype(BF16), vj)
            m_sc[mp] = m_new

    def body(j, carry):
        start = pl.multiple_of(j * tk, tk)
        step(k_ref[0, pl.ds(start, tk), :], v_ref[0, pl.ds(start, tk), :], None)
        return carry

    n_full = (qi * tq) // tk
    lax.fori_loop(0, n_full, body, 0)

    for d in range(tq // tk):
        start = pl.multiple_of(qi * tq + d * tk, tk)
        row = lax.broadcasted_iota(jnp.int32, (tq, tk), 0)
        col = lax.broadcasted_iota(jnp.int32, (tq, tk), 1) + d * tk
        step(k_ref[0, pl.ds(start, tk), :], v_ref[0, pl.ds(start, tk), :],
             (col // CHUNK) <= (row // CHUNK))

    lam = (jnp.exp(jnp.sum(lq1_ref[...] * lk1_ref[...], axis=-1, keepdims=True))
           - jnp.exp(jnp.sum(lq2_ref[...] * lk2_ref[...], axis=-1, keepdims=True)) + LAM_INIT)
    o = acc_sc[0] / l_sc[0] - lam * (acc_sc[1] / l_sc[1])
    o = _rms_rows(o, sub_ref[...]) * (1.0 - LAM_INIT)
    g = gate_ref[0].astype(F32)
    o_ref[0] = (o * (g * jax.nn.sigmoid(g))).astype(BF16)


def _attention(q, k, v, gate, p):
    bsz, seq, _ = q.shape
    tq = ATT_TQ
    qblk = pl.BlockSpec((1, tq, DIFF_DV), lambda b, h, i: (b, i, h))
    kvblk = pl.BlockSpec((1, seq, DIFF_DV), lambda b, h, i: (b, 0, h))
    vec = pl.BlockSpec((1, DIFF_DK), lambda b, h, i: (0, 0))
    return pl.pallas_call(
        _attn_kernel,
        grid=(bsz, DIFF_HEADS, seq // tq),
        in_specs=[qblk, kvblk, kvblk, qblk, vec, vec, vec, vec,
                  pl.BlockSpec((1, DIFF_DV), lambda b, h, i: (0, 0))],
        out_specs=qblk,
        out_shape=jax.ShapeDtypeStruct((bsz, seq, DIFF_HEADS * DIFF_DV), BF16),
        scratch_shapes=[
            pltpu.VMEM((2, tq, LANES), F32),
            pltpu.VMEM((2, tq, LANES), F32),
            pltpu.VMEM((2, tq, DIFF_DV), F32),
        ],
        compiler_params=pltpu.CompilerParams(
            dimension_semantics=("arbitrary", "arbitrary", "arbitrary"),
            vmem_limit_bytes=VMEM_LIMIT_BYTES),
        name="diff_attn",
    )(q, k, v, gate, p["lq1"], p["lk1"], p["lq2"], p["lk2"], p["subln_w"])


def _even_params(norm_w, w_in, conv_w, conv_b, dt_bias, a_log, d_skip, ssd_norm_w,
                 dw_w, dw_b, ln_w, ln_b, w_out):
    d_ssd = SSD_HEADS * SSD_HEADDIM
    xbc_dim = d_ssd + 2 * SSD_GROUPS * D_STATE
    o_za, o_xbc, o_dt = 0, d_ssd, d_ssd + xbc_dim
    o_gv = o_dt + SSD_HEADS
    o_gg, o_zb = o_gv + D_MODEL, o_gv + 2 * D_MODEL
    w_main = jnp.concatenate([
        w_in[:, o_xbc:o_xbc + xbc_dim], w_in[:, o_za:o_za + d_ssd],
        w_in[:, o_gv:o_gv + D_MODEL], w_in[:, o_gg:o_gg + D_MODEL], w_in[:, o_zb:o_zb + D_MODEL],
    ], axis=1).astype(BF16)
    pad = LANES - SSD_HEADS
    w_dt = jnp.pad(w_in[:, o_dt:o_dt + SSD_HEADS], ((0, 0), (0, pad))).astype(BF16)

    heads = np.arange(LANES)[:, None]
    e64 = (heads == (np.arange(d_ssd)[None, :] // SSD_HEADDIM)).astype(np.float32)
    e128 = (heads == (np.arange(SSD_HEADS * LANES)[None, :] // LANES)).astype(np.float32)
    r = np.arange(MIX_ROWS)
    same = (r[:, None] // CHUNK) == (r[None, :] // CHUNK)
    tri = (same & (r[None, :] <= r[:, None])).astype(np.float32)
    return dict(
        norm_w=norm_w[None, :], w_main=w_main, w_dt=w_dt,
        conv_w=conv_w, conv_b=conv_b[None, :],
        dt_bias=jnp.pad(dt_bias, (0, pad))[None, :], a_log=jnp.pad(a_log, (0, pad))[None, :],
        d_skip=jnp.repeat(d_skip, SSD_HEADDIM)[None, :], ssd_norm_w=ssd_norm_w[None, :],
        dw_w=dw_w, dw_b=dw_b[None, :], ln_w=ln_w[None, :], ln_b=ln_b[None, :],
        e64=jnp.asarray(e64, BF16), e128=jnp.asarray(e128, BF16),
        tri=jnp.asarray(tri, BF16), blk=jnp.asarray(same.astype(np.float32), BF16),
        w_out=w_out.astype(BF16),
    )


def _odd_params(norm_w, w_in, q_norm_w, k_norm_w, lq1, lk1, lq2, lk2, subln_w, w_out):
    maps = D_MODEL // DIFF_DK
    lane = np.arange(LANES) % DIFF_DK
    half = ROT_DIM // 2
    inv = (ROPE_THETA ** (-2.0 * jnp.arange(half, dtype=F32) / ROT_DIM))
    invf = jnp.where(jnp.asarray(lane < ROT_DIM), jnp.tile(inv, LANES // half), 0.0)[None, :]
    sgn = np.where(lane < half, -1.0, 1.0).astype(np.float32)[None, :]
    gblk = 2 * LANES
    gi = np.arange(gblk) // DIFF_DK
    gmean = (gi[:, None] == gi[None, :]).astype(np.float32) / DIFF_DK
    return dict(
        norm_w=norm_w[None, :], w_in=w_in.astype(BF16),
        q_norm_w=jnp.tile(q_norm_w, maps)[None, :], k_norm_w=jnp.tile(k_norm_w, maps)[None, :],
        invf=invf.astype(F32), sgn=jnp.asarray(sgn), gmean=jnp.asarray(gmean, BF16),
        lq1=lq1[None, :], lk1=lk1[None, :], lq2=lq2[None, :], lk2=lk2[None, :],
        subln_w=subln_w[None, :], w_out=w_out.astype(BF16),
    )


def kernel(x, positions, a_norm_w, a_w_in, a_conv_w, a_conv_b, a_dt_bias, a_a_log, a_d_skip,
           a_ssd_norm_w, a_dw_w, a_dw_b, a_ln_w, a_ln_b, a_w_out, c_norm_w, c_w_in, c_q_norm_w,
           c_k_norm_w, c_lq1, c_lk1, c_lq2, c_lk2, c_subln_w, c_w_out):
    bsz, seq, d = x.shape
    assert d == D_MODEL and seq % max(MIX_ROWS, ATT_TQ, PROJ_ROWS) == 0
    assert a_norm_w.shape[0] == 1 and c_norm_w.shape[0] == 1
    m = bsz * seq
    x2 = x.reshape(m, d)

    pe = _even_params(a_norm_w[0], a_w_in[0], a_conv_w[0], a_conv_b[0], a_dt_bias[0], a_a_log[0],
                      a_d_skip[0], a_ssd_norm_w[0], a_dw_w[0], a_dw_b[0], a_ln_w[0], a_ln_b[0], a_w_out[0])
    proj, dt_raw = _even_inproj(x2, pe["norm_w"], pe["w_main"], pe["w_dt"])
    y = _even_mix(proj, dt_raw, pe, bsz, seq)
    x2 = _outproj(y, pe["w_out"], x2, "even_outproj")

    po = _odd_params(c_norm_w[0], c_w_in[0], c_q_norm_w[0], c_k_norm_w[0], c_lq1[0], c_lk1[0],
                     c_lq2[0], c_lk2[0], c_subln_w[0], c_w_out[0])
    q, k, v, gate = _odd_inproj(x2, positions.reshape(m, 1), po)
    shp = (bsz, seq, d)
    o = _attention(q.reshape(shp), k.reshape(shp), v.reshape(shp), gate.reshape(shp), po)
    x2 = _outproj(o.reshape(m, d), po["w_out"], x2, "odd_outproj")
    return x2.reshape(bsz, seq, d)
```

```python
import functools
import math

import jax
import jax.numpy as jnp
import numpy as np
from jax import lax
from jax.experimental import pallas as pl
from jax.experimental.pallas import tpu as pltpu

F32 = jnp.float32
BF16 = jnp.bfloat16

D_MODEL = 1024
CHUNK = 64
SSD_HEADS = 16
SSD_HEADDIM = 64
SSD_GROUPS = 4
HEADS_PER_GROUP = SSD_HEADS // SSD_GROUPS
D_STATE = 128
SSD_CONV = 4
CONF_KERNEL = 31
DIFF_HEADS = 8
DIFF_DK = 64
DIFF_DV = 128
ROT_DIM = 16
ROPE_THETA = 500000.0
EPS = 1e-6
LN_EPS = 1e-5
ODD_LAYER_IDX = 1
LAM_INIT = 0.8 - 0.6 * math.exp(-0.3 * ODD_LAYER_IDX)

LANES = 128
SUBLANES = 8
VMEM_LIMIT_BYTES = 56 * 1024 * 1024

PROJ_ROWS = 512
MIX_ROWS = 256
PAIR = 2 * CHUNK
XCARRY = 8
UCARRY = 32
ATT_TQ = 512
ATT_TK = 512
NEG = -1e30


def _dot(a, b):
    return jnp.dot(a, b, preferred_element_type=F32)


def _dot_nt(a, b):
    return lax.dot_general(a, b, (((1,), (1,)), ((), ())), preferred_element_type=F32)


def _split_bf16(x, parts):
    out = []
    r = x
    for i in range(parts):
        h = r.astype(BF16)
        out.append(h)
        if i + 1 < parts:
            r = r - h.astype(F32)
    return out


def _expand(x, e_ref, parts=3):
    e = e_ref[...]
    acc = None
    for h in _split_bf16(x, parts):
        d = _dot(h, e)
        acc = d if acc is None else acc + d
    return acc


def _left_apply(m_ref, x, parts=3):
    m = m_ref[...]
    acc = None
    for h in _split_bf16(x, parts):
        d = _dot(m, h)
        acc = d if acc is None else acc + d
    return acc


def _rms_rows(x, w):
    ms = jnp.mean(x * x, axis=-1, keepdims=True)
    return x * lax.rsqrt(ms + EPS) * w


def _even_inproj_kernel(x_ref, nw_ref, w_ref, wdt_ref, proj_ref, dt_ref):
    hn = _rms_rows(x_ref[...], nw_ref[...]).astype(BF16)
    n_out = proj_ref.shape[1]
    for c in range(0, n_out, D_MODEL):
        proj_ref[:, c:c + D_MODEL] = _dot(hn, w_ref[:, c:c + D_MODEL]).astype(BF16)
    dt_ref[...] = _dot(hn, wdt_ref[...])


def _even_inproj(x2, nw, w, wdt):
    m = x2.shape[0]
    n_out = w.shape[1]
    tm = PROJ_ROWS
    return pl.pallas_call(
        _even_inproj_kernel,
        grid=(m // tm,),
        in_specs=[
            pl.BlockSpec((tm, D_MODEL), lambda i: (i, 0)),
            pl.BlockSpec((1, D_MODEL), lambda i: (0, 0)),
            pl.BlockSpec((D_MODEL, n_out), lambda i: (0, 0), pipeline_mode=pl.Buffered(1)),
            pl.BlockSpec((D_MODEL, LANES), lambda i: (0, 0), pipeline_mode=pl.Buffered(1)),
        ],
        out_specs=[
            pl.BlockSpec((tm, n_out), lambda i: (i, 0)),
            pl.BlockSpec((tm, LANES), lambda i: (i, 0)),
        ],
        out_shape=[
            jax.ShapeDtypeStruct((m, n_out), BF16),
            jax.ShapeDtypeStruct((m, LANES), F32),
        ],
        compiler_params=pltpu.CompilerParams(
            dimension_semantics=("arbitrary",), vmem_limit_bytes=VMEM_LIMIT_BYTES),
        name="even_inproj",
    )(x2, nw, w, wdt)


def _even_mix_kernel(xbc_ref, za_ref, gv_ref, gg_ref, zb_ref, dt_ref,
                     cw_ref, cb_ref, dtb_ref, alog_ref, dskip_ref, nrm_ref,
                     dww_ref, dwb_ref, lnw_ref, lnb_ref,
                     e64_ref, e128_ref, tri_ref, blk_ref,
                     out_ref,
                     xbuf, ubuf, hs, ybuf):
    t = pl.program_id(1)
    rows = za_ref.shape[0]
    d_ssd = SSD_HEADS * SSD_HEADDIM
    gw = HEADS_PER_GROUP * SSD_HEADDIM

    @pl.when(t == 0)
    def _():
        xbuf[0:XCARRY, :] = jnp.zeros((XCARRY, xbuf.shape[1]), F32)
        ubuf[0:UCARRY, :] = jnp.zeros((UCARRY, ubuf.shape[1]), F32)
        hs[...] = jnp.zeros(hs.shape, F32)

    xbuf[XCARRY:XCARRY + rows, :] = xbc_ref[...].astype(F32)
    acc = cb_ref[...]
    for k in range(SSD_CONV):
        off = XCARRY - (SSD_CONV - 1) + k
        acc = acc + cw_ref[k:k + 1, :] * xbuf[off:off + rows, :]
    xbuf[0:XCARRY, :] = xbuf[rows:rows + XCARRY, :]
    xc = acc * jax.nn.sigmoid(acc)
    xs = xc[:, :d_ssd]
    bm = xc[:, d_ssd:d_ssd + SSD_GROUPS * D_STATE]
    cm = xc[:, d_ssd + SSD_GROUPS * D_STATE:]

    dtr = dt_ref[...] + dtb_ref[...]
    dt = jnp.maximum(dtr, 0.0) + jnp.log1p(jnp.exp(-jnp.abs(dtr)))
    da = dt * (-jnp.exp(alog_ref[...]))
    cs = _left_apply(tri_ref, da)
    tot = _left_apply(blk_ref, da)
    dt_x = _expand(dt, e64_ref)
    ecs_x = _expand(jnp.exp(cs), e64_ref)
    edst_x = _expand(jnp.exp(tot - cs), e64_ref)
    ecd = jnp.exp(tot)
    cs_col = _expand(cs, e128_ref)
    cs_t = cs.T

    xd = xs * dt_x
    xdd = xd * edst_x

    li = lax.broadcasted_iota(jnp.int32, (PAIR, PAIR), 0)
    si = lax.broadcasted_iota(jnp.int32, (PAIR, PAIR), 1)
    intra = (si <= li) & ((si // CHUNK) == (li // CHUNK))
    lane_g = lax.broadcasted_iota(jnp.int32, (PAIR, gw), 1)
    row_g = lax.broadcasted_iota(jnp.int32, (PAIR, gw), 0)

    for pr in range(rows // PAIR):
        r0 = pr * PAIR
        for g in range(SSD_GROUPS):
            c_g = cm[r0:r0 + PAIR, g * D_STATE:(g + 1) * D_STATE].astype(BF16)
            b_g = bm[r0:r0 + PAIR, g * D_STATE:(g + 1) * D_STATE]
            cb = _dot_nt(c_g, b_g.astype(BF16))
            b_gt = b_g.T.astype(BF16)
            xd_g = xd[r0:r0 + PAIR, g * gw:(g + 1) * gw]
            xdd_g = xdd[r0:r0 + PAIR, g * gw:(g + 1) * gw]
            m_parts = []
            w_parts = []
            for e in range(HEADS_PER_GROUP):
                h = g * HEADS_PER_GROUP + e
                seg = cs_col[r0:r0 + PAIR, h * LANES:(h + 1) * LANES] - cs_t[h:h + 1, r0:r0 + PAIR]
                decay = jnp.exp(jnp.where(intra, seg, NEG))
                m_parts.append((cb * decay).astype(BF16))
                in_head = (lane_g // SSD_HEADDIM) == e
                w_parts.append(jnp.where(in_head, xd_g, 0.0).astype(BF16))
            y_diag = _dot(jnp.concatenate(m_parts, axis=1), jnp.concatenate(w_parts, axis=0))
            for j in range(PAIR // CHUNK):
                rj = r0 + j * CHUNK
                h_prev = hs[g]
                y_off = _dot(c_g[j * CHUNK:(j + 1) * CHUNK], h_prev.astype(BF16))
                y_off = y_off * ecs_x[rj:rj + CHUNK, g * gw:(g + 1) * gw]
                ybuf[rj:rj + CHUNK, g * gw:(g + 1) * gw] = y_diag[j * CHUNK:(j + 1) * CHUNK] + y_off
                in_chunk = (row_g // CHUNK) == j
                st = _dot(b_gt, jnp.where(in_chunk, xdd_g, 0.0).astype(BF16))
                ecd_x = _expand(ecd[rj:rj + SUBLANES], e64_ref)[0:1, g * gw:(g + 1) * gw]
                hs[g] = h_prev * ecd_x + st

    y = ybuf[...] + xs * dskip_ref[...]
    za = za_ref[...].astype(F32)
    y = y * (za * jax.nn.sigmoid(za))
    for g in range(SSD_GROUPS):
        yg = y[:, g * gw:(g + 1) * gw]
        out_ref[:, g * gw:(g + 1) * gw] = _rms_rows(yg, nrm_ref[:, g * gw:(g + 1) * gw]).astype(BF16)

    gg = gg_ref[...].astype(F32)
    ubuf[UCARRY:UCARRY + rows, :] = gv_ref[...].astype(F32) * jax.nn.sigmoid(gg)
    acc_u = dwb_ref[...]
    base = UCARRY - (CONF_KERNEL - 1)
    for sh in range(SUBLANES):
        taps = [k for k in range(CONF_KERNEL) if (base + k) % SUBLANES == sh]
        if not taps:
            continue
        span = (base + taps[-1] - sh) + rows
        shifted = ubuf[sh:sh + span, :]
        for k in taps:
            a0 = base + k - sh
            acc_u = acc_u + dww_ref[k:k + 1, :] * shifted[a0:a0 + rows, :]
    ubuf[0:UCARRY, :] = ubuf[rows:rows + UCARRY, :]
    mu = jnp.mean(acc_u, axis=-1, keepdims=True)
    cen = acc_u - mu
    var = jnp.mean(cen * cen, axis=-1, keepdims=True)
    un = cen * lax.rsqrt(var + LN_EPS) * lnw_ref[...] + lnb_ref[...]
    un = un * jax.nn.sigmoid(un)
    zb = zb_ref[...].astype(F32)
    out_ref[:, d_ssd:] = (un * (zb * jax.nn.sigmoid(zb))).astype(BF16)


def _even_mix(proj, dt_raw, p, bsz, seq):
    rows = MIX_ROWS
    nt = seq // rows
    d_ssd = SSD_HEADS * SSD_HEADDIM
    xbc_dim = d_ssd + 2 * SSD_GROUPS * D_STATE
    m = bsz * seq

    def rowblk(col):
        return lambda b, t: (b * nt + t, col)

    def const(shape):
        return pl.BlockSpec(shape, lambda b, t: (0, 0))

    in_specs = [
        pl.BlockSpec((rows, xbc_dim), rowblk(0)),
        pl.BlockSpec((rows, D_MODEL), rowblk(2)),
        pl.BlockSpec((rows, D_MODEL), rowblk(3)),
        pl.BlockSpec((rows, D_MODEL), rowblk(4)),
        pl.BlockSpec((rows, D_MODEL), rowblk(5)),
        pl.BlockSpec((rows, LANES), rowblk(0)),
        const((SSD_CONV, xbc_dim)), const((1, xbc_dim)),
        const((1, LANES)), const((1, LANES)),
        const((1, d_ssd)), const((1, d_ssd)),
        const((CONF_KERNEL, D_MODEL)), const((1, D_MODEL)), const((1, D_MODEL)), const((1, D_MODEL)),
        const((LANES, d_ssd)), const((LANES, SSD_HEADS * LANES)),
        const((rows, rows)), const((rows, rows)),
    ]
    return pl.pallas_call(
        _even_mix_kernel,
        grid=(bsz, nt),
        in_specs=in_specs,
        out_specs=pl.BlockSpec((rows, 2 * D_MODEL), rowblk(0)),
        out_shape=jax.ShapeDtypeStruct((m, 2 * D_MODEL), BF16),
        scratch_shapes=[
            pltpu.VMEM((XCARRY + rows, xbc_dim), F32),
            pltpu.VMEM((UCARRY + rows, D_MODEL), F32),
            pltpu.VMEM((SSD_GROUPS, D_STATE, HEADS_PER_GROUP * SSD_HEADDIM), F32),
            pltpu.VMEM((rows, d_ssd), F32),
        ],
        compiler_params=pltpu.CompilerParams(
            dimension_semantics=("arbitrary", "arbitrary"), vmem_limit_bytes=VMEM_LIMIT_BYTES),
        name="even_mix",
    )(proj, proj, proj, proj, proj, dt_raw,
      p["conv_w"], p["conv_b"], p["dt_bias"], p["a_log"], p["d_skip"], p["ssd_norm_w"],
      p["dw_w"], p["dw_b"], p["ln_w"], p["ln_b"],
      p["e64"], p["e128"], p["tri"], p["blk"])


def _outproj_kernel(y_ref, w_ref, x_ref, o_ref):
    o_ref[...] = x_ref[...] + _dot(y_ref[...], w_ref[...])


def _outproj(y, w, x2, name):
    m, kdim = y.shape
    tm = PROJ_ROWS
    return pl.pallas_call(
        _outproj_kernel,
        grid=(m // tm,),
        in_specs=[
            pl.BlockSpec((tm, kdim), lambda i: (i, 0)),
            pl.BlockSpec((kdim, D_MODEL), lambda i: (0, 0), pipeline_mode=pl.Buffered(1)),
            pl.BlockSpec((tm, D_MODEL), lambda i: (i, 0)),
        ],
        out_specs=pl.BlockSpec((tm, D_MODEL), lambda i: (i, 0)),
        out_shape=jax.ShapeDtypeStruct((m, D_MODEL), F32),
        compiler_params=pltpu.CompilerParams(
            dimension_semantics=("arbitrary",), vmem_limit_bytes=VMEM_LIMIT_BYTES),
        name=name,
    )(y, w, x2)


def _odd_inproj_kernel(x_ref, pos_ref, nw_ref, w_ref, wvt_ref, qnw_ref, knw_ref, invf_ref, sgn_ref,
                       gmean_ref, q_ref, k_ref, vt_ref, g_ref):
    hn = _rms_rows(x_ref[...], nw_ref[...]).astype(BF16)
    d = D_MODEL
    ang = pos_ref[...].astype(F32) * invf_ref[...]
    reps = d // LANES
    cos = jnp.concatenate([jnp.cos(ang)] * reps, axis=1)
    sin = jnp.concatenate([jnp.sin(ang) * sgn_ref[...]] * reps, axis=1)
    lane = lax.broadcasted_iota(jnp.int32, (1, d), 1) % DIFF_DK
    first_half = lane < (ROT_DIM // 2)
    gblk = gmean_ref.shape[0]

    def norm_rot(z, w_row, scale):
        sq = z * z
        parts = []
        for c in range(0, d, gblk):
            parts.append(_expand(sq[:, c:c + gblk], gmean_ref, parts=2))
        ms = jnp.concatenate(parts, axis=1)
        zn = z * lax.rsqrt(ms + EPS) * w_row
        swapped = jnp.where(first_half, pltpu.roll(zn, d - ROT_DIM // 2, 1), pltpu.roll(zn, ROT_DIM // 2, 1))
        out = zn * cos + swapped * sin
        if scale != 1.0:
            out = out * scale
        return out.astype(BF16)

    q = _dot(hn, w_ref[:, 0:d])
    q_ref[...] = norm_rot(q, qnw_ref[...], math.log2(math.e) * DIFF_DK ** -0.5)
    k = _dot(hn, w_ref[:, d:2 * d])
    k_ref[...] = norm_rot(k, knw_ref[...], 1.0)
    vt_ref[0] = _dot_nt(wvt_ref[...], hn).astype(BF16)
    g_ref[...] = _dot(hn, w_ref[:, 2 * d:3 * d]).astype(BF16)


def _odd_inproj(x2, pos2, p, bsz, seq):
    m = x2.shape[0]
    tm = PROJ_ROWS // 2
    nt = seq // tm
    row = lambda b, i: (b * nt + i, 0)
    const = lambda b, i: (0, 0)
    gblk = p["gmean"].shape[0]
    out_sds = jax.ShapeDtypeStruct((m, D_MODEL), BF16)
    return pl.pallas_call(
        _odd_inproj_kernel,
        grid=(bsz, nt),
        in_specs=[
            pl.BlockSpec((tm, D_MODEL), row),
            pl.BlockSpec((tm, 1), row),
            pl.BlockSpec((1, D_MODEL), const),
            pl.BlockSpec((D_MODEL, 3 * D_MODEL), const, pipeline_mode=pl.Buffered(1)),
            pl.BlockSpec((D_MODEL, D_MODEL), const, pipeline_mode=pl.Buffered(1)),
            pl.BlockSpec((1, D_MODEL), const),
            pl.BlockSpec((1, D_MODEL), const),
            pl.BlockSpec((1, LANES), const),
            pl.BlockSpec((1, LANES), const),
            pl.BlockSpec((gblk, gblk), const),
        ],
        out_specs=[pl.BlockSpec((tm, D_MODEL), row), pl.BlockSpec((tm, D_MODEL), row),
                   pl.BlockSpec((1, D_MODEL, tm), lambda b, i: (b, 0, i)),
                   pl.BlockSpec((tm, D_MODEL), row)],
        out_shape=[out_sds, out_sds, jax.ShapeDtypeStruct((bsz, D_MODEL, seq), BF16), out_sds],
        compiler_params=pltpu.CompilerParams(
            dimension_semantics=("arbitrary", "arbitrary"), vmem_limit_bytes=VMEM_LIMIT_BYTES),
        name="odd_inproj",
    )(x2, pos2, p["norm_w"], p["w_qkg"], p["w_vt"], p["q_norm_w"], p["k_norm_w"], p["invf"], p["sgn"],
      p["gmean"])


def _attn_kernel(q_ref, k_ref, vt_ref, gate_ref, lq1_ref, lk1_ref, lq2_ref, lk2_ref, sub_ref,
                 o_ref, acc_sc):
    qi = pl.program_id(2)
    tq = q_ref.shape[1]
    tk = ATT_TK
    q = q_ref[0]
    lane = lax.broadcasted_iota(jnp.int32, q.shape, 1)
    zero = jnp.zeros_like(q)
    q_maps = (jnp.where(lane < DIFF_DK, q, zero), jnp.where(lane >= DIFF_DK, q, zero))

    acc_sc[...] = jnp.zeros(acc_sc.shape, F32)

    def step(start, allowed, carry):
        kj = k_ref[0, pl.ds(start, tk), :]
        vtj = vt_ref[0, :, pl.ds(start, tk)]
        new = []
        for mp in range(2):
            m_prev, l_prev = carry[mp]
            s = _dot_nt(kj, q_maps[mp])
            if allowed is not None:
                s = jnp.where(allowed, s, NEG)
            m_new = jnp.maximum(m_prev, jnp.max(s, axis=0, keepdims=True))
            p = jnp.exp2(s - m_new)
            alpha = jnp.exp2(m_prev - m_new)
            l_new = alpha * l_prev + jnp.sum(p, axis=0, keepdims=True)
            acc_sc[mp] = alpha * acc_sc[mp] + _dot(vtj, p.astype(BF16))
            new.append((m_new, l_new))
        return tuple(new)

    def body(j, carry):
        return step(pl.multiple_of(j * tk, tk), None, carry)

    init = ((jnp.full((1, tq), NEG, F32), jnp.zeros((1, tq), F32)),) * 2
    carry = lax.fori_loop(0, (qi * tq) // tk, body, init)

    for d in range(tq // tk):
        key = lax.broadcasted_iota(jnp.int32, (tk, tq), 0) + d * tk
        qry = lax.broadcasted_iota(jnp.int32, (tk, tq), 1)
        carry = step(pl.multiple_of(qi * tq + d * tk, tk), (key // CHUNK) <= (qry // CHUNK), carry)

    lam = (jnp.exp(jnp.sum(lq1_ref[...] * lk1_ref[...], axis=-1, keepdims=True))
           - jnp.exp(jnp.sum(lq2_ref[...] * lk2_ref[...], axis=-1, keepdims=True)) + LAM_INIT)
    (_, l0), (_, l1) = carry
    o_t = acc_sc[0] / l0 - lam * (acc_sc[1] / l1)
    o = _rms_rows(o_t.T, sub_ref[...]) * (1.0 - LAM_INIT)
    g = gate_ref[0].astype(F32)
    o_ref[0] = (o * (g * jax.nn.sigmoid(g))).astype(BF16)


def _attention(q, k, vt, gate, p):
    bsz, seq, _ = q.shape
    tq = ATT_TQ
    qblk = pl.BlockSpec((1, tq, DIFF_DV), lambda b, h, i: (b, i, h))
    kblk = pl.BlockSpec((1, seq, DIFF_DV), lambda b, h, i: (b, 0, h))
    vtblk = pl.BlockSpec((1, DIFF_DV, seq), lambda b, h, i: (b, h, 0))
    vec = pl.BlockSpec((1, DIFF_DK), lambda b, h, i: (0, 0))
    return pl.pallas_call(
        _attn_kernel,
        grid=(bsz, DIFF_HEADS, seq // tq),
        in_specs=[qblk, kblk, vtblk, qblk, vec, vec, vec, vec,
                  pl.BlockSpec((1, DIFF_DV), lambda b, h, i: (0, 0))],
        out_specs=qblk,
        out_shape=jax.ShapeDtypeStruct((bsz, seq, DIFF_HEADS * DIFF_DV), BF16),
        scratch_shapes=[pltpu.VMEM((2, DIFF_DV, tq), F32)],
        compiler_params=pltpu.CompilerParams(
            dimension_semantics=("arbitrary", "arbitrary", "arbitrary"),
            vmem_limit_bytes=VMEM_LIMIT_BYTES),
        name="diff_attn",
    )(q, k, vt, gate, p["lq1"], p["lk1"], p["lq2"], p["lk2"], p["subln_w"])


def _even_params(norm_w, w_in, conv_w, conv_b, dt_bias, a_log, d_skip, ssd_norm_w,
                 dw_w, dw_b, ln_w, ln_b, w_out):
    d_ssd = SSD_HEADS * SSD_HEADDIM
    xbc_dim = d_ssd + 2 * SSD_GROUPS * D_STATE
    o_za, o_xbc, o_dt = 0, d_ssd, d_ssd + xbc_dim
    o_gv = o_dt + SSD_HEADS
    o_gg, o_zb = o_gv + D_MODEL, o_gv + 2 * D_MODEL
    w_main = jnp.concatenate([
        w_in[:, o_xbc:o_xbc + xbc_dim], w_in[:, o_za:o_za + d_ssd],
        w_in[:, o_gv:o_gv + D_MODEL], w_in[:, o_gg:o_gg + D_MODEL], w_in[:, o_zb:o_zb + D_MODEL],
    ], axis=1).astype(BF16)
    pad = LANES - SSD_HEADS
    w_dt = jnp.pad(w_in[:, o_dt:o_dt + SSD_HEADS], ((0, 0), (0, pad))).astype(BF16)

    heads = np.arange(LANES)[:, None]
    e64 = (heads == (np.arange(d_ssd)[None, :] // SSD_HEADDIM)).astype(np.float32)
    e128 = (heads == (np.arange(SSD_HEADS * LANES)[None, :] // LANES)).astype(np.float32)
    r = np.arange(MIX_ROWS)
    same = (r[:, None] // CHUNK) == (r[None, :] // CHUNK)
    tri = (same & (r[None, :] <= r[:, None])).astype(np.float32)
    return dict(
        norm_w=norm_w[None, :], w_main=w_main, w_dt=w_dt,
        conv_w=conv_w, conv_b=conv_b[None, :],
        dt_bias=jnp.pad(dt_bias, (0, pad))[None, :], a_log=jnp.pad(a_log, (0, pad))[None, :],
        d_skip=jnp.repeat(d_skip, SSD_HEADDIM)[None, :], ssd_norm_w=ssd_norm_w[None, :],
        dw_w=dw_w, dw_b=dw_b[None, :], ln_w=ln_w[None, :], ln_b=ln_b[None, :],
        e64=jnp.asarray(e64, BF16), e128=jnp.asarray(e128, BF16),
        tri=jnp.asarray(tri, BF16), blk=jnp.asarray(same.astype(np.float32), BF16),
        w_out=w_out.astype(BF16),
    )


def _odd_params(norm_w, w_in, q_norm_w, k_norm_w, lq1, lk1, lq2, lk2, subln_w, w_out):
    maps = D_MODEL // DIFF_DK
    lane = np.arange(LANES) % DIFF_DK
    half = ROT_DIM // 2
    inv = (ROPE_THETA ** (-2.0 * jnp.arange(half, dtype=F32) / ROT_DIM))
    invf = jnp.where(jnp.asarray(lane < ROT_DIM), jnp.tile(inv, LANES // half), 0.0)[None, :]
    sgn = np.where(lane < half, -1.0, 1.0).astype(np.float32)[None, :]
    gblk = 2 * LANES
    gi = np.arange(gblk) // DIFF_DK
    gmean = (gi[:, None] == gi[None, :]).astype(np.float32) / DIFF_DK
    return dict(
        norm_w=norm_w[None, :],
        w_qkg=jnp.concatenate([w_in[:, :2 * D_MODEL], w_in[:, 3 * D_MODEL:]], axis=1).astype(BF16),
        w_vt=w_in[:, 2 * D_MODEL:3 * D_MODEL].T.astype(BF16),
        q_norm_w=jnp.tile(q_norm_w, maps)[None, :], k_norm_w=jnp.tile(k_norm_w, maps)[None, :],
        invf=invf.astype(F32), sgn=jnp.asarray(sgn), gmean=jnp.asarray(gmean, BF16),
        lq1=lq1[None, :], lk1=lk1[None, :], lq2=lq2[None, :], lk2=lk2[None, :],
        subln_w=subln_w[None, :], w_out=w_out.astype(BF16),
    )


def kernel(x, positions, a_norm_w, a_w_in, a_conv_w, a_conv_b, a_dt_bias, a_a_log, a_d_skip,
           a_ssd_norm_w, a_dw_w, a_dw_b, a_ln_w, a_ln_b, a_w_out, c_norm_w, c_w_in, c_q_norm_w,
           c_k_norm_w, c_lq1, c_lk1, c_lq2, c_lk2, c_subln_w, c_w_out):
    bsz, seq, d = x.shape
    assert d == D_MODEL and seq % max(MIX_ROWS, ATT_TQ, PROJ_ROWS) == 0
    assert a_norm_w.shape[0] == 1 and c_norm_w.shape[0] == 1
    m = bsz * seq
    x2 = x.reshape(m, d)

    pe = _even_params(a_norm_w[0], a_w_in[0], a_conv_w[0], a_conv_b[0], a_dt_bias[0], a_a_log[0],
                      a_d_skip[0], a_ssd_norm_w[0], a_dw_w[0], a_dw_b[0], a_ln_w[0], a_ln_b[0], a_w_out[0])
    proj, dt_raw = _even_inproj(x2, pe["norm_w"], pe["w_main"], pe["w_dt"])
    y = _even_mix(proj, dt_raw, pe, bsz, seq)
    x2 = _outproj(y, pe["w_out"], x2, "even_outproj")

    po = _odd_params(c_norm_w[0], c_w_in[0], c_q_norm_w[0], c_k_norm_w[0], c_lq1[0], c_lk1[0],
                     c_lq2[0], c_lk2[0], c_subln_w[0], c_w_out[0])
    q, k, vt, gate = _odd_inproj(x2, positions.reshape(m, 1), po, bsz, seq)
    shp = (bsz, seq, d)
    o = _attention(q.reshape(shp), k.reshape(shp), vt, gate.reshape(shp), po)
    x2 = _outproj(o.reshape(m, d), po["w_out"], x2, "odd_outproj")
    return x2.reshape(bsz, seq, d)
```

```python
import functools
import math

import jax
import jax.numpy as jnp
import numpy as np
from jax import lax
from jax.experimental import pallas as pl
from jax.experimental.pallas import tpu as pltpu

F32 = jnp.float32
BF16 = jnp.bfloat16

D_MODEL = 1024
CHUNK = 64
SSD_HEADS = 16
SSD_HEADDIM = 64
SSD_GROUPS = 4
HEADS_PER_GROUP = SSD_HEADS // SSD_GROUPS
D_STATE = 128
SSD_CONV = 4
CONF_KERNEL = 31
DIFF_HEADS = 8
DIFF_DK = 64
DIFF_DV = 128
ROT_DIM = 16
ROPE_THETA = 500000.0
EPS = 1e-6
LN_EPS = 1e-5
ODD_LAYER_IDX = 1
LAM_INIT = 0.8 - 0.6 * math.exp(-0.3 * ODD_LAYER_IDX)

LANES = 128
SUBLANES = 8
VMEM_LIMIT_BYTES = 56 * 1024 * 1024

PROJ_ROWS = 512
MIX_ROWS = 256
PAIR = 2 * CHUNK
XCARRY = 8
UCARRY = 32
CONV_ROWS = 128
ATT_TQ = 512
ATT_TK = 512
NEG = -1e30


def _dot(a, b):
    return jnp.dot(a, b, preferred_element_type=F32)


def _dot_nt(a, b):
    return lax.dot_general(a, b, (((1,), (1,)), ((), ())), preferred_element_type=F32)


def _split2(x):
    hi = x.astype(BF16)
    lo = (x - hi.astype(F32)).astype(BF16)
    return hi, lo


def _expand(x, e2_ref):
    hi, lo = _split2(x)
    return _dot(jnp.concatenate([hi, lo], axis=1), e2_ref[...])


def _left_apply(m_ref, x):
    hi, lo = _split2(x)
    n = x.shape[1]
    r = _dot(m_ref[...], jnp.concatenate([hi, lo], axis=1))
    return r[:, :n] + r[:, n:]


def _rms_rows(x, w):
    ms = jnp.mean(x * x, axis=-1, keepdims=True)
    return x * lax.rsqrt(ms + EPS) * w


def _even_inproj_kernel(x_ref, nw_ref, w_ref, wdt_ref, proj_ref, dt_ref):
    hn = _rms_rows(x_ref[...], nw_ref[...]).astype(BF16)
    n_out = proj_ref.shape[1]
    for c in range(0, n_out, D_MODEL):
        proj_ref[:, c:c + D_MODEL] = _dot(hn, w_ref[:, c:c + D_MODEL]).astype(BF16)
    dt_ref[...] = _dot(hn, wdt_ref[...])


def _even_inproj(x2, nw, w, wdt):
    m = x2.shape[0]
    n_out = w.shape[1]
    tm = PROJ_ROWS
    return pl.pallas_call(
        _even_inproj_kernel,
        grid=(m // tm,),
        in_specs=[
            pl.BlockSpec((tm, D_MODEL), lambda i: (i, 0)),
            pl.BlockSpec((1, D_MODEL), lambda i: (0, 0)),
            pl.BlockSpec((D_MODEL, n_out), lambda i: (0, 0), pipeline_mode=pl.Buffered(1)),
            pl.BlockSpec((D_MODEL, LANES), lambda i: (0, 0), pipeline_mode=pl.Buffered(1)),
        ],
        out_specs=[
            pl.BlockSpec((tm, n_out), lambda i: (i, 0)),
            pl.BlockSpec((tm, LANES), lambda i: (i, 0)),
        ],
        out_shape=[
            jax.ShapeDtypeStruct((m, n_out), BF16),
            jax.ShapeDtypeStruct((m, LANES), F32),
        ],
        compiler_params=pltpu.CompilerParams(
            dimension_semantics=("arbitrary",), vmem_limit_bytes=VMEM_LIMIT_BYTES),
        name="even_inproj",
    )(x2, nw, w, wdt)


def _even_mix_kernel(xbc_ref, za_ref, gv_ref, gg_ref, zb_ref, dt_ref,
                     cw_ref, cb_ref, dtb_ref, alog_ref, dskip_ref, nrm_ref,
                     dww_ref, dwb_ref, lnw_ref, lnb_ref,
                     e64_ref, e128_ref, tri_ref, blk_ref,
                     out_ref,
                     xbuf, ubuf, hs, ybuf, uph, cbuf):
    t = pl.program_id(1)
    rows = za_ref.shape[0]
    d_ssd = SSD_HEADS * SSD_HEADDIM
    gw = HEADS_PER_GROUP * SSD_HEADDIM

    @pl.when(t == 0)
    def _():
        xbuf[0:XCARRY, :] = jnp.zeros((XCARRY, xbuf.shape[1]), F32)
        ubuf[0:UCARRY, :] = jnp.zeros((UCARRY, ubuf.shape[1]), F32)
        hs[...] = jnp.zeros(hs.shape, F32)

    xbuf[XCARRY:XCARRY + rows, :] = xbc_ref[...].astype(F32)
    acc = cb_ref[...]
    for k in range(SSD_CONV):
        off = XCARRY - (SSD_CONV - 1) + k
        acc = acc + cw_ref[k:k + 1, :] * xbuf[off:off + rows, :]
    xbuf[0:XCARRY, :] = xbuf[rows:rows + XCARRY, :]
    xc = acc * jax.nn.sigmoid(acc)
    xs = xc[:, :d_ssd]
    bm = xc[:, d_ssd:d_ssd + SSD_GROUPS * D_STATE]
    cm = xc[:, d_ssd + SSD_GROUPS * D_STATE:]

    dtr = dt_ref[...] + dtb_ref[...]
    dt = jnp.maximum(dtr, 0.0) + jnp.log1p(jnp.exp(-jnp.abs(dtr)))
    da = dt * (-jnp.exp(alog_ref[...]))
    cs = _left_apply(tri_ref, da)
    tot = _left_apply(blk_ref, da)
    dt_x = _expand(dt, e64_ref)
    ecs_x = _expand(jnp.exp(cs), e64_ref)
    edst_x = _expand(jnp.exp(tot - cs), e64_ref)
    ecd_x = _expand(jnp.exp(tot), e64_ref)
    cs_col = _expand(cs, e128_ref)
    cs_t = cs.T

    xd = xs * dt_x
    xdd = xd * edst_x

    li = lax.broadcasted_iota(jnp.int32, (PAIR, PAIR), 0)
    si = lax.broadcasted_iota(jnp.int32, (PAIR, PAIR), 1)
    intra = (si <= li) & ((si // CHUNK) == (li // CHUNK))
    lane_g = lax.broadcasted_iota(jnp.int32, (PAIR, gw), 1)
    row_g = lax.broadcasted_iota(jnp.int32, (PAIR, gw), 0)

    for pr in range(rows // PAIR):
        r0 = pr * PAIR
        for g in range(SSD_GROUPS):
            c_g = cm[r0:r0 + PAIR, g * D_STATE:(g + 1) * D_STATE].astype(BF16)
            b_g = bm[r0:r0 + PAIR, g * D_STATE:(g + 1) * D_STATE]
            cb = _dot_nt(c_g, b_g.astype(BF16))
            b_gt = b_g.T.astype(BF16)
            xd_g = xd[r0:r0 + PAIR, g * gw:(g + 1) * gw]
            xdd_g = xdd[r0:r0 + PAIR, g * gw:(g + 1) * gw]
            m_parts = []
            w_parts = []
            for e in range(HEADS_PER_GROUP):
                h = g * HEADS_PER_GROUP + e
                seg = cs_col[r0:r0 + PAIR, h * LANES:(h + 1) * LANES] - cs_t[h:h + 1, r0:r0 + PAIR]
                decay = jnp.exp(jnp.where(intra, seg, NEG))
                m_parts.append((cb * decay).astype(BF16))
                in_head = (lane_g // SSD_HEADDIM) == e
                w_parts.append(jnp.where(in_head, xd_g, 0.0).astype(BF16))
            y_diag = _dot(jnp.concatenate(m_parts, axis=1), jnp.concatenate(w_parts, axis=0))
            for j in range(PAIR // CHUNK):
                rj = r0 + j * CHUNK
                h_prev = hs[g]
                y_off = _dot(c_g[j * CHUNK:(j + 1) * CHUNK], h_prev.astype(BF16))
                y_off = y_off * ecs_x[rj:rj + CHUNK, g * gw:(g + 1) * gw]
                ybuf[rj:rj + CHUNK, g * gw:(g + 1) * gw] = y_diag[j * CHUNK:(j + 1) * CHUNK] + y_off
                in_chunk = (row_g // CHUNK) == j
                st = _dot(b_gt, jnp.where(in_chunk, xdd_g, 0.0).astype(BF16))
                hs[g] = h_prev * ecd_x[rj:rj + 1, g * gw:(g + 1) * gw] + st

    y = ybuf[...] + xs * dskip_ref[...]
    za = za_ref[...].astype(F32)
    y = y * (za * jax.nn.sigmoid(za))
    for g in range(SSD_GROUPS):
        yg = y[:, g * gw:(g + 1) * gw]
        out_ref[:, g * gw:(g + 1) * gw] = _rms_rows(yg, nrm_ref[:, g * gw:(g + 1) * gw]).astype(BF16)

    gg = gg_ref[...].astype(F32)
    ubuf[UCARRY:UCARRY + rows, :] = gv_ref[...].astype(F32) * jax.nn.sigmoid(gg)
    base = UCARRY - (CONF_KERNEL - 1)
    ntile = D_MODEL // LANES
    for sh in range(SUBLANES):
        span = UCARRY + rows - (SUBLANES if sh else 0)
        for c in range(ntile):
            uph[sh, c, 0:span, :] = ubuf[sh:sh + span, c * LANES:(c + 1) * LANES]
    ubuf[0:UCARRY, :] = ubuf[rows:rows + UCARRY, :]
    nblk = rows // CONV_ROWS

    def conv_block(i, carry):
        c = i // nblk
        r0 = pl.multiple_of((i % nblk) * CONV_ROWS, CONV_ROWS)
        nsub = CONV_ROWS // SUBLANES
        a = jnp.broadcast_to(dwb_ref[c], (nsub, SUBLANES, LANES))
        for k in range(CONF_KERNEL):
            sh = (base + k) % SUBLANES
            start = pl.multiple_of(r0 + (base + k - sh), SUBLANES)
            blk = uph[sh, c, pl.ds(start, CONV_ROWS), :].reshape(nsub, SUBLANES, LANES)
            a = a + dww_ref[k, c] * blk
        cbuf[c, pl.ds(r0, CONV_ROWS), :] = a.reshape(CONV_ROWS, LANES)
        return carry

    lax.fori_loop(0, ntile * nblk, conv_block, 0)
    acc_u = jnp.concatenate([cbuf[c] for c in range(ntile)], axis=1)
    mu = jnp.mean(acc_u, axis=-1, keepdims=True)
    cen = acc_u - mu
    var = jnp.mean(cen * cen, axis=-1, keepdims=True)
    un = cen * lax.rsqrt(var + LN_EPS) * lnw_ref[...] + lnb_ref[...]
    un = un * jax.nn.sigmoid(un)
    zb = zb_ref[...].astype(F32)
    out_ref[:, d_ssd:] = (un * (zb * jax.nn.sigmoid(zb))).astype(BF16)


def _even_mix(proj, dt_raw, p, bsz, seq):
    rows = MIX_ROWS
    nt = seq // rows
    d_ssd = SSD_HEADS * SSD_HEADDIM
    xbc_dim = d_ssd + 2 * SSD_GROUPS * D_STATE
    m = bsz * seq

    def rowblk(col):
        return lambda b, t: (b * nt + t, col)

    def const(shape):
        return pl.BlockSpec(shape, lambda b, t: (0, 0))

    in_specs = [
        pl.BlockSpec((rows, xbc_dim), rowblk(0)),
        pl.BlockSpec((rows, D_MODEL), rowblk(2)),
        pl.BlockSpec((rows, D_MODEL), rowblk(3)),
        pl.BlockSpec((rows, D_MODEL), rowblk(4)),
        pl.BlockSpec((rows, D_MODEL), rowblk(5)),
        pl.BlockSpec((rows, LANES), rowblk(0)),
        const((SSD_CONV, xbc_dim)), const((1, xbc_dim)),
        const((1, LANES)), const((1, LANES)),
        const((1, d_ssd)), const((1, d_ssd)),
        pl.BlockSpec((CONF_KERNEL, D_MODEL // LANES, SUBLANES, LANES), lambda b, t: (0, 0, 0, 0)),
        pl.BlockSpec((D_MODEL // LANES, SUBLANES, LANES), lambda b, t: (0, 0, 0)),
        const((1, D_MODEL)), const((1, D_MODEL)),
        const((2 * LANES, d_ssd)), const((2 * LANES, SSD_HEADS * LANES)),
        const((rows, rows)), const((rows, rows)),
    ]
    return pl.pallas_call(
        _even_mix_kernel,
        grid=(bsz, nt),
        in_specs=in_specs,
        out_specs=pl.BlockSpec((rows, 2 * D_MODEL), rowblk(0)),
        out_shape=jax.ShapeDtypeStruct((m, 2 * D_MODEL), BF16),
        scratch_shapes=[
            pltpu.VMEM((XCARRY + rows, xbc_dim), F32),
            pltpu.VMEM((UCARRY + rows, D_MODEL), F32),
            pltpu.VMEM((SSD_GROUPS, D_STATE, HEADS_PER_GROUP * SSD_HEADDIM), F32),
            pltpu.VMEM((rows, d_ssd), F32),
            pltpu.VMEM((SUBLANES, D_MODEL // LANES, UCARRY + rows, LANES), F32),
            pltpu.VMEM((D_MODEL // LANES, rows, LANES), F32),
        ],
        compiler_params=pltpu.CompilerParams(
            dimension_semantics=("arbitrary", "arbitrary"), vmem_limit_bytes=VMEM_LIMIT_BYTES),
        name="even_mix",
    )(proj, proj, proj, proj, proj, dt_raw,
      p["conv_w"], p["conv_b"], p["dt_bias"], p["a_log"], p["d_skip"], p["ssd_norm_w"],
      p["dw_w"], p["dw_b"], p["ln_w"], p["ln_b"],
      p["e64"], p["e128"], p["tri"], p["blk"])


def _outproj_kernel(y_ref, w_ref, x_ref, o_ref):
    o_ref[...] = x_ref[...] + _dot(y_ref[...], w_ref[...])


def _outproj(y, w, x2, name):
    m, kdim = y.shape
    tm = PROJ_ROWS
    return pl.pallas_call(
        _outproj_kernel,
        grid=(m // tm,),
        in_specs=[
            pl.BlockSpec((tm, kdim), lambda i: (i, 0)),
            pl.BlockSpec((kdim, D_MODEL), lambda i: (0, 0), pipeline_mode=pl.Buffered(1)),
            pl.BlockSpec((tm, D_MODEL), lambda i: (i, 0)),
        ],
        out_specs=pl.BlockSpec((tm, D_MODEL), lambda i: (i, 0)),
        out_shape=jax.ShapeDtypeStruct((m, D_MODEL), F32),
        compiler_params=pltpu.CompilerParams(
            dimension_semantics=("arbitrary",), vmem_limit_bytes=VMEM_LIMIT_BYTES),
        name=name,
    )(y, w, x2)


def _odd_inproj_kernel(x_ref, pos_ref, nw_ref, w_ref, wvt_ref, qnw_ref, knw_ref, invf_ref, sgn_ref,
                       gmean_ref, q_ref, k_ref, vt_ref, g_ref):
    hn = _rms_rows(x_ref[...], nw_ref[...]).astype(BF16)
    d = D_MODEL
    ang = pos_ref[...].astype(F32) * invf_ref[...]
    reps = d // LANES
    cos = jnp.concatenate([jnp.cos(ang)] * reps, axis=1)
    sin = jnp.concatenate([jnp.sin(ang) * sgn_ref[...]] * reps, axis=1)
    lane = lax.broadcasted_iota(jnp.int32, (1, d), 1) % DIFF_DK
    first_half = lane < (ROT_DIM // 2)
    gblk = gmean_ref.shape[1]

    def norm_rot(z, w_row, scale):
        sq = z * z
        parts = []
        for c in range(0, d, gblk):
            parts.append(_expand(sq[:, c:c + gblk], gmean_ref))
        ms = jnp.concatenate(parts, axis=1)
        zn = z * lax.rsqrt(ms + EPS) * w_row
        swapped = jnp.where(first_half, pltpu.roll(zn, d - ROT_DIM // 2, 1), pltpu.roll(zn, ROT_DIM // 2, 1))
        out = zn * cos + swapped * sin
        if scale != 1.0:
            out = out * scale
        return out.astype(BF16)

    q = _dot(hn, w_ref[:, 0:d])
    q_ref[...] = norm_rot(q, qnw_ref[...], math.log2(math.e) * DIFF_DK ** -0.5)
    k = _dot(hn, w_ref[:, d:2 * d])
    k_ref[...] = norm_rot(k, knw_ref[...], 1.0)
    vt_ref[0] = _dot_nt(wvt_ref[...], hn).astype(BF16)
    g_ref[...] = _dot(hn, w_ref[:, 2 * d:3 * d]).astype(BF16)


def _odd_inproj(x2, pos2, p, bsz, seq):
    m = x2.shape[0]
    tm = PROJ_ROWS // 2
    nt = seq // tm
    row = lambda b, i: (b * nt + i, 0)
    const = lambda b, i: (0, 0)
    gblk = p["gmean"].shape[1]
    out_sds = jax.ShapeDtypeStruct((m, D_MODEL), BF16)
    return pl.pallas_call(
        _odd_inproj_kernel,
        grid=(bsz, nt),
        in_specs=[
            pl.BlockSpec((tm, D_MODEL), row),
            pl.BlockSpec((tm, 1), row),
            pl.BlockSpec((1, D_MODEL), const),
            pl.BlockSpec((D_MODEL, 3 * D_MODEL), const, pipeline_mode=pl.Buffered(1)),
            pl.BlockSpec((D_MODEL, D_MODEL), const, pipeline_mode=pl.Buffered(1)),
            pl.BlockSpec((1, D_MODEL), const),
            pl.BlockSpec((1, D_MODEL), const),
            pl.BlockSpec((1, LANES), const),
            pl.BlockSpec((1, LANES), const),
            pl.BlockSpec((2 * gblk, gblk), const),
        ],
        out_specs=[pl.BlockSpec((tm, D_MODEL), row), pl.BlockSpec((tm, D_MODEL), row),
                   pl.BlockSpec((1, D_MODEL, tm), lambda b, i: (b, 0, i)),
                   pl.BlockSpec((tm, D_MODEL), row)],
        out_shape=[out_sds, out_sds, jax.ShapeDtypeStruct((bsz, D_MODEL, seq), BF16), out_sds],
        compiler_params=pltpu.CompilerParams(
            dimension_semantics=("arbitrary", "arbitrary"), vmem_limit_bytes=VMEM_LIMIT_BYTES),
        name="odd_inproj",
    )(x2, pos2, p["norm_w"], p["w_qkg"], p["w_vt"], p["q_norm_w"], p["k_norm_w"], p["invf"], p["sgn"],
      p["gmean"])


def _attn_kernel(q_ref, k_ref, vt_ref, gate_ref, lq1_ref, lk1_ref, lq2_ref, lk2_ref, sub_ref,
                 o_ref, acc_sc, s_a, s_b, tmax_sc, ml_sc):
    qi = pl.program_id(2)
    tq = q_ref.shape[1]
    tk = ATT_TK
    assert tq == tk
    q = q_ref[0]
    lane = lax.broadcasted_iota(jnp.int32, q.shape, 1)
    zero = jnp.zeros_like(q)
    q_maps = (jnp.where(lane < DIFF_DK, q, zero), jnp.where(lane >= DIFF_DK, q, zero))

    acc_sc[...] = jnp.zeros(acc_sc.shape, F32)
    ml_sc[0] = jnp.full(ml_sc.shape[1:], NEG, F32)
    ml_sc[1] = jnp.zeros(ml_sc.shape[1:], F32)

    def tile_start(u):
        return pl.multiple_of(jnp.where(u == 0, qi, u - 1) * tk, tk)

    def scores(u, s_buf, slot, masked):
        kj = k_ref[0, pl.ds(tile_start(u), tk), :]
        for mp in range(2):
            s = _dot_nt(kj, q_maps[mp])
            if masked:
                key = lax.broadcasted_iota(jnp.int32, (tk, tq), 0)
                qry = lax.broadcasted_iota(jnp.int32, (tk, tq), 1)
                s = jnp.where((key // CHUNK) <= (qry // CHUNK), s, NEG)
            s_buf[mp] = s
            tmax_sc[slot, mp] = jnp.max(s, axis=0, keepdims=True)

    def absorb(u, s_buf, slot):
        vtj = vt_ref[0, :, pl.ds(tile_start(u), tk)]
        for mp in range(2):
            m_prev = ml_sc[0, mp]
            m_new = jnp.maximum(m_prev, tmax_sc[slot, mp])
            p = jnp.exp2(s_buf[mp] - m_new)
            alpha = jnp.exp2(m_prev - m_new)
            ml_sc[1, mp] = alpha * ml_sc[1, mp] + jnp.sum(p, axis=0, keepdims=True)
            acc_sc[mp] = alpha * acc_sc[mp] + _dot(vtj, p.astype(BF16))
            ml_sc[0, mp] = m_new

    scores(0, s_a, 0, True)

    def body(t, carry):
        u = 2 * t
        scores(u + 1, s_b, 1, False)
        absorb(u, s_a, 0)
        scores(u + 2, s_a, 0, False)
        absorb(u + 1, s_b, 1)
        return carry

    lax.fori_loop(0, qi // 2, body, 0)

    @pl.when(qi % 2 == 1)
    def _():
        scores(qi, s_b, 1, False)
        absorb(qi - 1, s_a, 0)
        absorb(qi, s_b, 1)

    @pl.when(qi % 2 == 0)
    def _():
        absorb(qi, s_a, 0)

    lam = (jnp.exp(jnp.sum(lq1_ref[...] * lk1_ref[...], axis=-1, keepdims=True))
           - jnp.exp(jnp.sum(lq2_ref[...] * lk2_ref[...], axis=-1, keepdims=True)) + LAM_INIT)
    o_t = acc_sc[0] / ml_sc[1, 0] - lam * (acc_sc[1] / ml_sc[1, 1])
    o = _rms_rows(o_t.T, sub_ref[...]) * (1.0 - LAM_INIT)
    g = gate_ref[0].astype(F32)
    o_ref[0] = (o * (g * jax.nn.sigmoid(g))).astype(BF16)


def _attention(q, k, vt, gate, p):
    bsz, seq, _ = q.shape
    tq = ATT_TQ
    qblk = pl.BlockSpec((1, tq, DIFF_DV), lambda b, h, i: (b, i, h))
    kblk = pl.BlockSpec((1, seq, DIFF_DV), lambda b, h, i: (b, 0, h))
    vtblk = pl.BlockSpec((1, DIFF_DV, seq), lambda b, h, i: (b, h, 0))
    vec = pl.BlockSpec((1, DIFF_DK), lambda b, h, i: (0, 0))
    return pl.pallas_call(
        _attn_kernel,
        grid=(bsz, DIFF_HEADS, seq // tq),
        in_specs=[qblk, kblk, vtblk, qblk, vec, vec, vec, vec,
                  pl.BlockSpec((1, DIFF_DV), lambda b, h, i: (0, 0))],
        out_specs=qblk,
        out_shape=jax.ShapeDtypeStruct((bsz, seq, DIFF_HEADS * DIFF_DV), BF16),
        scratch_shapes=[
            pltpu.VMEM((2, DIFF_DV, tq), F32),
            pltpu.VMEM((2, ATT_TK, tq), F32),
            pltpu.VMEM((2, ATT_TK, tq), F32),
            pltpu.VMEM((2, 2, 1, tq), F32),
            pltpu.VMEM((2, 2, 1, tq), F32),
        ],
        compiler_params=pltpu.CompilerParams(
            dimension_semantics=("arbitrary", "arbitrary", "arbitrary"),
            vmem_limit_bytes=VMEM_LIMIT_BYTES),
        name="diff_attn",
    )(q, k, vt, gate, p["lq1"], p["lk1"], p["lq2"], p["lk2"], p["subln_w"])


def _even_params(norm_w, w_in, conv_w, conv_b, dt_bias, a_log, d_skip, ssd_norm_w,
                 dw_w, dw_b, ln_w, ln_b, w_out):
    d_ssd = SSD_HEADS * SSD_HEADDIM
    xbc_dim = d_ssd + 2 * SSD_GROUPS * D_STATE
    o_za, o_xbc, o_dt = 0, d_ssd, d_ssd + xbc_dim
    o_gv = o_dt + SSD_HEADS
    o_gg, o_zb = o_gv + D_MODEL, o_gv + 2 * D_MODEL
    w_main = jnp.concatenate([
        w_in[:, o_xbc:o_xbc + xbc_dim], w_in[:, o_za:o_za + d_ssd],
        w_in[:, o_gv:o_gv + D_MODEL], w_in[:, o_gg:o_gg + D_MODEL], w_in[:, o_zb:o_zb + D_MODEL],
    ], axis=1).astype(BF16)
    pad = LANES - SSD_HEADS
    w_dt = jnp.pad(w_in[:, o_dt:o_dt + SSD_HEADS], ((0, 0), (0, pad))).astype(BF16)

    heads = np.arange(LANES)[:, None]
    e64 = (heads == (np.arange(d_ssd)[None, :] // SSD_HEADDIM)).astype(np.float32)
    e128 = (heads == (np.arange(SSD_HEADS * LANES)[None, :] // LANES)).astype(np.float32)
    r = np.arange(MIX_ROWS)
    same = (r[:, None] // CHUNK) == (r[None, :] // CHUNK)
    tri = (same & (r[None, :] <= r[:, None])).astype(np.float32)
    return dict(
        norm_w=norm_w[None, :], w_main=w_main, w_dt=w_dt,
        conv_w=conv_w, conv_b=conv_b[None, :],
        dt_bias=jnp.pad(dt_bias, (0, pad))[None, :], a_log=jnp.pad(a_log, (0, pad))[None, :],
        d_skip=jnp.repeat(d_skip, SSD_HEADDIM)[None, :], ssd_norm_w=ssd_norm_w[None, :],
        dw_w=jnp.broadcast_to(dw_w.reshape(CONF_KERNEL, D_MODEL // LANES, 1, LANES),
                              (CONF_KERNEL, D_MODEL // LANES, SUBLANES, LANES)),
        dw_b=jnp.broadcast_to(dw_b.reshape(D_MODEL // LANES, 1, LANES), (D_MODEL // LANES, SUBLANES, LANES)),
        ln_w=ln_w[None, :], ln_b=ln_b[None, :],
        e64=jnp.asarray(np.concatenate([e64, e64]), BF16), e128=jnp.asarray(np.concatenate([e128, e128]), BF16),
        tri=jnp.asarray(tri, BF16), blk=jnp.asarray(same.astype(np.float32), BF16),
        w_out=w_out.astype(BF16),
    )


def _odd_params(norm_w, w_in, q_norm_w, k_norm_w, lq1, lk1, lq2, lk2, subln_w, w_out):
    maps = D_MODEL // DIFF_DK
    lane = np.arange(LANES) % DIFF_DK
    half = ROT_DIM // 2
    inv = (ROPE_THETA ** (-2.0 * jnp.arange(half, dtype=F32) / ROT_DIM))
    invf = jnp.where(jnp.asarray(lane < ROT_DIM), jnp.tile(inv, LANES // half), 0.0)[None, :]
    sgn = np.where(lane < half, -1.0, 1.0).astype(np.float32)[None, :]
    gblk = 2 * LANES
    gi = np.arange(gblk) // DIFF_DK
    gmean = (gi[:, None] == gi[None, :]).astype(np.float32) / DIFF_DK
    return dict(
        norm_w=norm_w[None, :],
        w_qkg=jnp.concatenate([w_in[:, :2 * D_MODEL], w_in[:, 3 * D_MODEL:]], axis=1).astype(BF16),
        w_vt=w_in[:, 2 * D_MODEL:3 * D_MODEL].T.astype(BF16),
        q_norm_w=jnp.tile(q_norm_w, maps)[None, :], k_norm_w=jnp.tile(k_norm_w, maps)[None, :],
        invf=invf.astype(F32), sgn=jnp.asarray(sgn), gmean=jnp.asarray(np.concatenate([gmean, gmean]), BF16),
        lq1=lq1[None, :], lk1=lk1[None, :], lq2=lq2[None, :], lk2=lk2[None, :],
        subln_w=subln_w[None, :], w_out=w_out.astype(BF16),
    )


def kernel(x, positions, a_norm_w, a_w_in, a_conv_w, a_conv_b, a_dt_bias, a_a_log, a_d_skip,
           a_ssd_norm_w, a_dw_w, a_dw_b, a_ln_w, a_ln_b, a_w_out, c_norm_w, c_w_in, c_q_norm_w,
           c_k_norm_w, c_lq1, c_lk1, c_lq2, c_lk2, c_subln_w, c_w_out):
    bsz, seq, d = x.shape
    assert d == D_MODEL and seq % max(MIX_ROWS, ATT_TQ, PROJ_ROWS) == 0
    assert a_norm_w.shape[0] == 1 and c_norm_w.shape[0] == 1
    m = bsz * seq
    x2 = x.reshape(m, d)

    pe = _even_params(a_norm_w[0], a_w_in[0], a_conv_w[0], a_conv_b[0], a_dt_bias[0], a_a_log[0],
                      a_d_skip[0], a_ssd_norm_w[0], a_dw_w[0], a_dw_b[0], a_ln_w[0], a_ln_b[0], a_w_out[0])
    proj, dt_raw = _even_inproj(x2, pe["norm_w"], pe["w_main"], pe["w_dt"])
    y = _even_mix(proj, dt_raw, pe, bsz, seq)
    x2 = _outproj(y, pe["w_out"], x2, "even_outproj")

    po = _odd_params(c_norm_w[0], c_w_in[0], c_q_norm_w[0], c_k_norm_w[0], c_lq1[0], c_lk1[0],
                     c_lq2[0], c_lk2[0], c_subln_w[0], c_w_out[0])
    q, k, vt, gate = _odd_inproj(x2, positions.reshape(m, 1), po, bsz, seq)
    shp = (bsz, seq, d)
    o = _attention(q.reshape(shp), k.reshape(shp), vt, gate.reshape(shp), po)
    x2 = _outproj(o.reshape(m, d), po["w_out"], x2, "odd_outproj")
    return x2.reshape(bsz, seq, d)
```

```python
import functools
import math

import jax
import jax.numpy as jnp
import numpy as np
from jax import lax
from jax.experimental import pallas as pl
from jax.experimental.pallas import tpu as pltpu

F32 = jnp.float32
BF16 = jnp.bfloat16

D_MODEL = 1024
CHUNK = 64
SSD_HEADS = 16
SSD_HEADDIM = 64
SSD_GROUPS = 4
HEADS_PER_GROUP = SSD_HEADS // SSD_GROUPS
D_STATE = 128
SSD_CONV = 4
CONF_KERNEL = 31
DIFF_HEADS = 8
DIFF_DK = 64
DIFF_DV = 128
ROT_DIM = 16
ROPE_THETA = 500000.0
EPS = 1e-6
LN_EPS = 1e-5
ODD_LAYER_IDX = 1
LAM_INIT = 0.8 - 0.6 * math.exp(-0.3 * ODD_LAYER_IDX)

LANES = 128
SUBLANES = 8
VMEM_LIMIT_BYTES = 56 * 1024 * 1024

PROJ_ROWS = 512
MIX_ROWS = 256
PAIR = 2 * CHUNK
XCARRY = 8
UCARRY = 32
CONV_ROWS = 128
ATT_TQ = 512
ATT_TK = 512
ATT_UNROLL = 4
NEG = -1e30


def _dot(a, b):
    return jnp.dot(a, b, preferred_element_type=F32)


def _dot_nt(a, b):
    return lax.dot_general(a, b, (((1,), (1,)), ((), ())), preferred_element_type=F32)


def _split2(x):
    hi = x.astype(BF16)
    lo = (x - hi.astype(F32)).astype(BF16)
    return hi, lo


def _expand(x, e2_ref):
    hi, lo = _split2(x)
    return _dot(jnp.concatenate([hi, lo], axis=1), e2_ref[...])


def _left_apply(m_ref, x):
    hi, lo = _split2(x)
    n = x.shape[1]
    r = _dot(m_ref[...], jnp.concatenate([hi, lo], axis=1))
    return r[:, :n] + r[:, n:]


def _rms_rows(x, w):
    ms = jnp.mean(x * x, axis=-1, keepdims=True)
    return x * lax.rsqrt(ms + EPS) * w


def _even_inproj_kernel(x_ref, nw_ref, w_ref, wdt_ref, proj_ref, dt_ref):
    hn = _rms_rows(x_ref[...], nw_ref[...]).astype(BF16)
    n_out = proj_ref.shape[1]
    for c in range(0, n_out, D_MODEL):
        proj_ref[:, c:c + D_MODEL] = _dot(hn, w_ref[:, c:c + D_MODEL]).astype(BF16)
    dt_ref[...] = _dot(hn, wdt_ref[...])


def _even_inproj(x2, nw, w, wdt):
    m = x2.shape[0]
    n_out = w.shape[1]
    tm = PROJ_ROWS
    return pl.pallas_call(
        _even_inproj_kernel,
        grid=(m // tm,),
        in_specs=[
            pl.BlockSpec((tm, D_MODEL), lambda i: (i, 0)),
            pl.BlockSpec((1, D_MODEL), lambda i: (0, 0)),
            pl.BlockSpec((D_MODEL, n_out), lambda i: (0, 0), pipeline_mode=pl.Buffered(1)),
            pl.BlockSpec((D_MODEL, LANES), lambda i: (0, 0), pipeline_mode=pl.Buffered(1)),
        ],
        out_specs=[
            pl.BlockSpec((tm, n_out), lambda i: (i, 0)),
            pl.BlockSpec((tm, LANES), lambda i: (i, 0)),
        ],
        out_shape=[
            jax.ShapeDtypeStruct((m, n_out), BF16),
            jax.ShapeDtypeStruct((m, LANES), F32),
        ],
        compiler_params=pltpu.CompilerParams(
            dimension_semantics=("arbitrary",), vmem_limit_bytes=VMEM_LIMIT_BYTES),
        name="even_inproj",
    )(x2, nw, w, wdt)


def _even_mix_kernel(xbc_ref, za_ref, gv_ref, gg_ref, zb_ref, dt_ref,
                     cw_ref, cb_ref, dtb_ref, alog_ref, dskip_ref, nrm_ref,
                     dww_ref, dwb_ref, lnw_ref, lnb_ref,
                     e64_ref, e128_ref, tri_ref, blk_ref,
                     out_ref,
                     xbuf, ubuf, hs, ybuf, uph, cbuf):
    t = pl.program_id(1)
    rows = za_ref.shape[0]
    d_ssd = SSD_HEADS * SSD_HEADDIM
    gw = HEADS_PER_GROUP * SSD_HEADDIM

    @pl.when(t == 0)
    def _():
        xbuf[0:XCARRY, :] = jnp.zeros((XCARRY, xbuf.shape[1]), F32)
        ubuf[0:UCARRY, :] = jnp.zeros((UCARRY, ubuf.shape[1]), F32)
        hs[...] = jnp.zeros(hs.shape, F32)

    xbuf[XCARRY:XCARRY + rows, :] = xbc_ref[...].astype(F32)
    acc = cb_ref[...]
    for k in range(SSD_CONV):
        off = XCARRY - (SSD_CONV - 1) + k
        acc = acc + cw_ref[k:k + 1, :] * xbuf[off:off + rows, :]
    xbuf[0:XCARRY, :] = xbuf[rows:rows + XCARRY, :]
    xc = acc * jax.nn.sigmoid(acc)
    xs = xc[:, :d_ssd]
    bm = xc[:, d_ssd:d_ssd + SSD_GROUPS * D_STATE]
    cm = xc[:, d_ssd + SSD_GROUPS * D_STATE:]

    dtr = dt_ref[...] + dtb_ref[...]
    dt = jnp.maximum(dtr, 0.0) + jnp.log1p(jnp.exp(-jnp.abs(dtr)))
    da = dt * (-jnp.exp(alog_ref[...]))
    cs = _left_apply(tri_ref, da)
    tot = _left_apply(blk_ref, da)
    dt_x = _expand(dt, e64_ref)
    ecs_x = _expand(jnp.exp(cs), e64_ref)
    edst_x = _expand(jnp.exp(tot - cs), e64_ref)
    ecd_x = _expand(jnp.exp(tot), e64_ref)
    cs_col = _expand(cs, e128_ref)
    cs_t = cs.T

    xd = xs * dt_x
    xdd = xd * edst_x

    li = lax.broadcasted_iota(jnp.int32, (PAIR, PAIR), 0)
    si = lax.broadcasted_iota(jnp.int32, (PAIR, PAIR), 1)
    intra = (si <= li) & ((si // CHUNK) == (li // CHUNK))
    lane_g = lax.broadcasted_iota(jnp.int32, (PAIR, gw), 1)
    row_g = lax.broadcasted_iota(jnp.int32, (PAIR, gw), 0)

    for pr in range(rows // PAIR):
        r0 = pr * PAIR
        for g in range(SSD_GROUPS):
            c_g = cm[r0:r0 + PAIR, g * D_STATE:(g + 1) * D_STATE].astype(BF16)
            b_g = bm[r0:r0 + PAIR, g * D_STATE:(g + 1) * D_STATE]
            cb = _dot_nt(c_g, b_g.astype(BF16))
            b_gt = b_g.T.astype(BF16)
            xd_g = xd[r0:r0 + PAIR, g * gw:(g + 1) * gw]
            xdd_g = xdd[r0:r0 + PAIR, g * gw:(g + 1) * gw]
            m_parts = []
            w_parts = []
            for e in range(HEADS_PER_GROUP):
                h = g * HEADS_PER_GROUP + e
                seg = cs_col[r0:r0 + PAIR, h * LANES:(h + 1) * LANES] - cs_t[h:h + 1, r0:r0 + PAIR]
                decay = jnp.exp(jnp.where(intra, seg, NEG))
                m_parts.append((cb * decay).astype(BF16))
                in_head = (lane_g // SSD_HEADDIM) == e
                w_parts.append(jnp.where(in_head, xd_g, 0.0).astype(BF16))
            y_diag = _dot(jnp.concatenate(m_parts, axis=1), jnp.concatenate(w_parts, axis=0))
            for j in range(PAIR // CHUNK):
                rj = r0 + j * CHUNK
                h_prev = hs[g]
                y_off = _dot(c_g[j * CHUNK:(j + 1) * CHUNK], h_prev.astype(BF16))
                y_off = y_off * ecs_x[rj:rj + CHUNK, g * gw:(g + 1) * gw]
                ybuf[rj:rj + CHUNK, g * gw:(g + 1) * gw] = y_diag[j * CHUNK:(j + 1) * CHUNK] + y_off
                in_chunk = (row_g // CHUNK) == j
                st = _dot(b_gt, jnp.where(in_chunk, xdd_g, 0.0).astype(BF16))
                hs[g] = h_prev * ecd_x[rj:rj + 1, g * gw:(g + 1) * gw] + st

    y = ybuf[...] + xs * dskip_ref[...]
    za = za_ref[...].astype(F32)
    y = y * (za * jax.nn.sigmoid(za))
    for g in range(SSD_GROUPS):
        yg = y[:, g * gw:(g + 1) * gw]
        out_ref[:, g * gw:(g + 1) * gw] = _rms_rows(yg, nrm_ref[:, g * gw:(g + 1) * gw]).astype(BF16)

    gg = gg_ref[...].astype(F32)
    ubuf[UCARRY:UCARRY + rows, :] = gv_ref[...].astype(F32) * jax.nn.sigmoid(gg)
    base = UCARRY - (CONF_KERNEL - 1)
    ntile = D_MODEL // LANES
    for sh in range(SUBLANES):
        span = UCARRY + rows - (SUBLANES if sh else 0)
        for c in range(ntile):
            uph[sh, c, 0:span, :] = ubuf[sh:sh + span, c * LANES:(c + 1) * LANES]
    ubuf[0:UCARRY, :] = ubuf[rows:rows + UCARRY, :]
    nblk = rows // CONV_ROWS

    def conv_block(i, carry):
        c = i // nblk
        r0 = pl.multiple_of((i % nblk) * CONV_ROWS, CONV_ROWS)
        nsub = CONV_ROWS // SUBLANES
        a = jnp.broadcast_to(dwb_ref[c], (nsub, SUBLANES, LANES))
        for k in range(CONF_KERNEL):
            sh = (base + k) % SUBLANES
            start = pl.multiple_of(r0 + (base + k - sh), SUBLANES)
            blk = uph[sh, c, pl.ds(start, CONV_ROWS), :].reshape(nsub, SUBLANES, LANES)
            a = a + dww_ref[k, c] * blk
        cbuf[c, pl.ds(r0, CONV_ROWS), :] = a.reshape(CONV_ROWS, LANES)
        return carry

    lax.fori_loop(0, ntile * nblk, conv_block, 0)
    acc_u = jnp.concatenate([cbuf[c] for c in range(ntile)], axis=1)
    mu = jnp.mean(acc_u, axis=-1, keepdims=True)
    cen = acc_u - mu
    var = jnp.mean(cen * cen, axis=-1, keepdims=True)
    un = cen * lax.rsqrt(var + LN_EPS) * lnw_ref[...] + lnb_ref[...]
    un = un * jax.nn.sigmoid(un)
    zb = zb_ref[...].astype(F32)
    out_ref[:, d_ssd:] = (un * (zb * jax.nn.sigmoid(zb))).astype(BF16)


def _even_mix(proj, dt_raw, p, bsz, seq):
    rows = MIX_ROWS
    nt = seq // rows
    d_ssd = SSD_HEADS * SSD_HEADDIM
    xbc_dim = d_ssd + 2 * SSD_GROUPS * D_STATE
    m = bsz * seq

    def rowblk(col):
        return lambda b, t: (b * nt + t, col)

    def const(shape):
        return pl.BlockSpec(shape, lambda b, t: (0, 0))

    in_specs = [
        pl.BlockSpec((rows, xbc_dim), rowblk(0)),
        pl.BlockSpec((rows, D_MODEL), rowblk(2)),
        pl.BlockSpec((rows, D_MODEL), rowblk(3)),
        pl.BlockSpec((rows, D_MODEL), rowblk(4)),
        pl.BlockSpec((rows, D_MODEL), rowblk(5)),
        pl.BlockSpec((rows, LANES), rowblk(0)),
        const((SSD_CONV, xbc_dim)), const((1, xbc_dim)),
        const((1, LANES)), const((1, LANES)),
        const((1, d_ssd)), const((1, d_ssd)),
        pl.BlockSpec((CONF_KERNEL, D_MODEL // LANES, SUBLANES, LANES), lambda b, t: (0, 0, 0, 0)),
        pl.BlockSpec((D_MODEL // LANES, SUBLANES, LANES), lambda b, t: (0, 0, 0)),
        const((1, D_MODEL)), const((1, D_MODEL)),
        const((2 * LANES, d_ssd)), const((2 * LANES, SSD_HEADS * LANES)),
        const((rows, rows)), const((rows, rows)),
    ]
    return pl.pallas_call(
        _even_mix_kernel,
        grid=(bsz, nt),
        in_specs=in_specs,
        out_specs=pl.BlockSpec((rows, 2 * D_MODEL), rowblk(0)),
        out_shape=jax.ShapeDtypeStruct((m, 2 * D_MODEL), BF16),
        scratch_shapes=[
            pltpu.VMEM((XCARRY + rows, xbc_dim), F32),
            pltpu.VMEM((UCARRY + rows, D_MODEL), F32),
            pltpu.VMEM((SSD_GROUPS, D_STATE, HEADS_PER_GROUP * SSD_HEADDIM), F32),
            pltpu.VMEM((rows, d_ssd), F32),
            pltpu.VMEM((SUBLANES, D_MODEL // LANES, UCARRY + rows, LANES), F32),
            pltpu.VMEM((D_MODEL // LANES, rows, LANES), F32),
        ],
        compiler_params=pltpu.CompilerParams(
            dimension_semantics=("arbitrary", "arbitrary"), vmem_limit_bytes=VMEM_LIMIT_BYTES),
        name="even_mix",
    )(proj, proj, proj, proj, proj, dt_raw,
      p["conv_w"], p["conv_b"], p["dt_bias"], p["a_log"], p["d_skip"], p["ssd_norm_w"],
      p["dw_w"], p["dw_b"], p["ln_w"], p["ln_b"],
      p["e64"], p["e128"], p["tri"], p["blk"])


def _outproj_kernel(y_ref, w_ref, x_ref, o_ref):
    o_ref[...] = x_ref[...] + _dot(y_ref[...], w_ref[...])


def _outproj(y, w, x2, name):
    m, kdim = y.shape
    tm = PROJ_ROWS
    return pl.pallas_call(
        _outproj_kernel,
        grid=(m // tm,),
        in_specs=[
            pl.BlockSpec((tm, kdim), lambda i: (i, 0)),
            pl.BlockSpec((kdim, D_MODEL), lambda i: (0, 0), pipeline_mode=pl.Buffered(1)),
            pl.BlockSpec((tm, D_MODEL), lambda i: (i, 0)),
        ],
        out_specs=pl.BlockSpec((tm, D_MODEL), lambda i: (i, 0)),
        out_shape=jax.ShapeDtypeStruct((m, D_MODEL), F32),
        compiler_params=pltpu.CompilerParams(
            dimension_semantics=("arbitrary",), vmem_limit_bytes=VMEM_LIMIT_BYTES),
        name=name,
    )(y, w, x2)


def _odd_inproj_kernel(x_ref, pos_ref, nw_ref, w_ref, wvt_ref, qnw_ref, knw_ref, invf_ref, sgn_ref,
                       sel_ref, one_ref, gmean_ref, q_ref, k_ref, vt_ref, g_ref):
    hn = _rms_rows(x_ref[...], nw_ref[...]).astype(BF16)
    d = D_MODEL
    half = ROT_DIM // 2
    ang_t = invf_ref[...] * pos_ref[0].astype(F32)
    pad = jnp.zeros((LANES - half, ang_t.shape[1]), F32)
    cos_c = jnp.concatenate([jnp.cos(ang_t), pad], axis=0).T
    sin_c = jnp.concatenate([jnp.sin(ang_t), pad], axis=0).T
    reps = d // LANES
    cos = jnp.concatenate([_expand(cos_c, sel_ref) + one_ref[...]] * reps, axis=1)
    sin = jnp.concatenate([_expand(sin_c, sel_ref) * sgn_ref[...]] * reps, axis=1)
    lane = lax.broadcasted_iota(jnp.int32, (1, d), 1) % DIFF_DK
    first_half = lane < (ROT_DIM // 2)
    gblk = gmean_ref.shape[0]

    def norm_rot(z, w_row, scale):
        sq = (z * z).astype(BF16)
        parts = []
        for c in range(0, d, gblk):
            parts.append(_dot(sq[:, c:c + gblk], gmean_ref[...]))
        ms = jnp.concatenate(parts, axis=1)
        zn = z * lax.rsqrt(ms + EPS) * w_row
        swapped = jnp.where(first_half, pltpu.roll(zn, d - ROT_DIM // 2, 1), pltpu.roll(zn, ROT_DIM // 2, 1))
        out = zn * cos + swapped * sin
        if scale != 1.0:
            out = out * scale
        return out.astype(BF16)

    q = _dot(hn, w_ref[:, 0:d])
    k = _dot(hn, w_ref[:, d:2 * d])
    q_ref[...] = norm_rot(q, qnw_ref[...], math.log2(math.e) * DIFF_DK ** -0.5)
    vt_ref[0] = _dot_nt(wvt_ref[...], hn).astype(BF16)
    k_ref[...] = norm_rot(k, knw_ref[...], 1.0)
    g_ref[...] = _dot(hn, w_ref[:, 2 * d:3 * d]).astype(BF16)


def _odd_inproj(x2, pos2, p, bsz, seq):
    m = x2.shape[0]
    tm = PROJ_ROWS
    nt = seq // tm
    row = lambda b, i: (b * nt + i, 0)
    const = lambda b, i: (0, 0)
    gblk = p["gmean"].shape[0]
    out_sds = jax.ShapeDtypeStruct((m, D_MODEL), BF16)
    return pl.pallas_call(
        _odd_inproj_kernel,
        grid=(bsz, nt),
        in_specs=[
            pl.BlockSpec((tm, D_MODEL), row),
            pl.BlockSpec((1, 1, tm), lambda b, i: (b * nt + i, 0, 0)),
            pl.BlockSpec((1, D_MODEL), const),
            pl.BlockSpec((D_MODEL, 3 * D_MODEL), const, pipeline_mode=pl.Buffered(1)),
            pl.BlockSpec((D_MODEL, D_MODEL), const, pipeline_mode=pl.Buffered(1)),
            pl.BlockSpec((1, D_MODEL), const),
            pl.BlockSpec((1, D_MODEL), const),
            pl.BlockSpec((ROT_DIM // 2, 1), const),
            pl.BlockSpec((1, LANES), const),
            pl.BlockSpec((2 * LANES, LANES), const),
            pl.BlockSpec((1, LANES), const),
            pl.BlockSpec((gblk, gblk), const),
        ],
        out_specs=[pl.BlockSpec((tm, D_MODEL), row), pl.BlockSpec((tm, D_MODEL), row),
                   pl.BlockSpec((1, D_MODEL, tm), lambda b, i: (b, 0, i)),
                   pl.BlockSpec((tm, D_MODEL), row)],
        out_shape=[out_sds, out_sds, jax.ShapeDtypeStruct((bsz, D_MODEL, seq), BF16), out_sds],
        compiler_params=pltpu.CompilerParams(
            dimension_semantics=("arbitrary", "arbitrary"), vmem_limit_bytes=VMEM_LIMIT_BYTES),
        name="odd_inproj",
    )(x2, pos2.reshape(m // tm, 1, tm), p["norm_w"], p["w_qkg"], p["w_vt"], p["q_norm_w"], p["k_norm_w"],
      p["invf"], p["sgn"], p["sel"], p["one"], p["gmean"])


def _attn_kernel(q_ref, k_ref, vt_ref, gate_ref, lq1_ref, lk1_ref, lq2_ref, lk2_ref, sub_ref,
                 o_ref, acc_sc, s_a, s_b, tmax_sc, ml_sc):
    qi = pl.program_id(2)
    tq = q_ref.shape[1]
    tk = ATT_TK
    assert tq == tk
    q = q_ref[0]
    lane = lax.broadcasted_iota(jnp.int32, q.shape, 1)
    zero = jnp.zeros_like(q)
    q_maps = (jnp.where(lane < DIFF_DK, q, zero), jnp.where(lane >= DIFF_DK, q, zero))

    acc_sc[...] = jnp.zeros(acc_sc.shape, F32)
    ml_sc[0] = jnp.full(ml_sc.shape[1:], NEG, F32)
    ml_sc[1] = jnp.zeros(ml_sc.shape[1:], F32)

    def tile_start(u):
        return pl.multiple_of(jnp.where(u == 0, qi, u - 1) * tk, tk)

    def scores(u, s_buf, slot, masked):
        kj = k_ref[0, pl.ds(tile_start(u), tk), :]
        for mp in range(2):
            s = _dot_nt(kj, q_maps[mp])
            if masked:
                key = lax.broadcasted_iota(jnp.int32, (tk, tq), 0)
                qry = lax.broadcasted_iota(jnp.int32, (tk, tq), 1)
                s = jnp.where((key // CHUNK) <= (qry // CHUNK), s, NEG)
            s_buf[mp] = s
            tmax_sc[slot, mp] = jnp.max(s, axis=0, keepdims=True)

    def absorb(u, s_buf, slot):
        vtj = vt_ref[0, :, pl.ds(tile_start(u), tk)]
        for mp in range(2):
            m_prev = ml_sc[0, mp]
            m_new = jnp.maximum(m_prev, tmax_sc[slot, mp])
            p = jnp.exp2(s_buf[mp] - m_new)
            alpha = jnp.exp2(m_prev - m_new)
            ml_sc[1, mp] = alpha * ml_sc[1, mp] + jnp.sum(p, axis=0, keepdims=True)
            acc_sc[mp] = alpha * acc_sc[mp] + _dot(vtj, p.astype(BF16))
            ml_sc[0, mp] = m_new

    scores(0, s_a, 0, True)

    bufs = ((s_a, 0), (s_b, 1))
    assert ATT_UNROLL % 2 == 0

    def steps(u0, count, tail):
        for j in range(count):
            nxt, cur = bufs[(j + 1) % 2], bufs[j % 2]
            scores(u0 + j + 1, nxt[0], nxt[1], False)
            absorb(u0 + j, cur[0], cur[1])
        if tail:
            last = bufs[count % 2]
            absorb(u0 + count, last[0], last[1])

    def body(t, carry):
        steps(ATT_UNROLL * t, ATT_UNROLL, False)
        return carry

    lax.fori_loop(0, qi // ATT_UNROLL, body, 0)
    done = (qi // ATT_UNROLL) * ATT_UNROLL
    for r in range(ATT_UNROLL):
        @pl.when(qi - done == r)
        def _(r=r):
            steps(done, r, True)

    lam = (jnp.exp(jnp.sum(lq1_ref[...] * lk1_ref[...], axis=-1, keepdims=True))
           - jnp.exp(jnp.sum(lq2_ref[...] * lk2_ref[...], axis=-1, keepdims=True)) + LAM_INIT)
    o_t = acc_sc[0] / ml_sc[1, 0] - lam * (acc_sc[1] / ml_sc[1, 1])
    o = _rms_rows(o_t.T, sub_ref[...]) * (1.0 - LAM_INIT)
    g = gate_ref[0].astype(F32)
    o_ref[0] = (o * (g * jax.nn.sigmoid(g))).astype(BF16)


def _attention(q, k, vt, gate, p):
    bsz, seq, _ = q.shape
    tq = ATT_TQ
    qblk = pl.BlockSpec((1, tq, DIFF_DV), lambda b, h, i: (b, i, h))
    kblk = pl.BlockSpec((1, seq, DIFF_DV), lambda b, h, i: (b, 0, h))
    vtblk = pl.BlockSpec((1, DIFF_DV, seq), lambda b, h, i: (b, h, 0))
    vec = pl.BlockSpec((1, DIFF_DK), lambda b, h, i: (0, 0))
    return pl.pallas_call(
        _attn_kernel,
        grid=(bsz, DIFF_HEADS, seq // tq),
        in_specs=[qblk, kblk, vtblk, qblk, vec, vec, vec, vec,
                  pl.BlockSpec((1, DIFF_DV), lambda b, h, i: (0, 0))],
        out_specs=qblk,
        out_shape=jax.ShapeDtypeStruct((bsz, seq, DIFF_HEADS * DIFF_DV), BF16),
        scratch_shapes=[
            pltpu.VMEM((2, DIFF_DV, tq), F32),
            pltpu.VMEM((2, ATT_TK, tq), F32),
            pltpu.VMEM((2, ATT_TK, tq), F32),
            pltpu.VMEM((2, 2, 1, tq), F32),
            pltpu.VMEM((2, 2, 1, tq), F32),
        ],
        compiler_params=pltpu.CompilerParams(
            dimension_semantics=("arbitrary", "arbitrary", "arbitrary"),
            vmem_limit_bytes=VMEM_LIMIT_BYTES),
        name="diff_attn",
    )(q, k, vt, gate, p["lq1"], p["lk1"], p["lq2"], p["lk2"], p["subln_w"])


def _even_params(norm_w, w_in, conv_w, conv_b, dt_bias, a_log, d_skip, ssd_norm_w,
                 dw_w, dw_b, ln_w, ln_b, w_out):
    d_ssd = SSD_HEADS * SSD_HEADDIM
    xbc_dim = d_ssd + 2 * SSD_GROUPS * D_STATE
    o_za, o_xbc, o_dt = 0, d_ssd, d_ssd + xbc_dim
    o_gv = o_dt + SSD_HEADS
    o_gg, o_zb = o_gv + D_MODEL, o_gv + 2 * D_MODEL
    w_main = jnp.concatenate([
        w_in[:, o_xbc:o_xbc + xbc_dim], w_in[:, o_za:o_za + d_ssd],
        w_in[:, o_gv:o_gv + D_MODEL], w_in[:, o_gg:o_gg + D_MODEL], w_in[:, o_zb:o_zb + D_MODEL],
    ], axis=1).astype(BF16)
    pad = LANES - SSD_HEADS
    w_dt = jnp.pad(w_in[:, o_dt:o_dt + SSD_HEADS], ((0, 0), (0, pad))).astype(BF16)

    heads = np.arange(LANES)[:, None]
    e64 = (heads == (np.arange(d_ssd)[None, :] // SSD_HEADDIM)).astype(np.float32)
    e128 = (heads == (np.arange(SSD_HEADS * LANES)[None, :] // LANES)).astype(np.float32)
    r = np.arange(MIX_ROWS)
    same = (r[:, None] // CHUNK) == (r[None, :] // CHUNK)
    tri = (same & (r[None, :] <= r[:, None])).astype(np.float32)
    return dict(
        norm_w=norm_w[None, :], w_main=w_main, w_dt=w_dt,
        conv_w=conv_w, conv_b=conv_b[None, :],
        dt_bias=jnp.pad(dt_bias, (0, pad))[None, :], a_log=jnp.pad(a_log, (0, pad))[None, :],
        d_skip=jnp.repeat(d_skip, SSD_HEADDIM)[None, :], ssd_norm_w=ssd_norm_w[None, :],
        dw_w=jnp.broadcast_to(dw_w.reshape(CONF_KERNEL, D_MODEL // LANES, 1, LANES),
                              (CONF_KERNEL, D_MODEL // LANES, SUBLANES, LANES)),
        dw_b=jnp.broadcast_to(dw_b.reshape(D_MODEL // LANES, 1, LANES), (D_MODEL // LANES, SUBLANES, LANES)),
        ln_w=ln_w[None, :], ln_b=ln_b[None, :],
        e64=jnp.asarray(np.concatenate([e64, e64]), BF16), e128=jnp.asarray(np.concatenate([e128, e128]), BF16),
        tri=jnp.asarray(tri, BF16), blk=jnp.asarray(same.astype(np.float32), BF16),
        w_out=w_out.astype(BF16),
    )


def _odd_params(norm_w, w_in, q_norm_w, k_norm_w, lq1, lk1, lq2, lk2, subln_w, w_out):
    maps = D_MODEL // DIFF_DK
    lane = np.arange(LANES) % DIFF_DK
    half = ROT_DIM // 2
    inv = (ROPE_THETA ** (-2.0 * jnp.arange(half, dtype=F32) / ROT_DIM))
    rot = lane < ROT_DIM
    sgn = np.where(lane < half, -1.0, 1.0).astype(np.float32)[None, :]
    sel = ((np.arange(LANES)[:, None] == (lane % half)[None, :]) & rot[None, :]).astype(np.float32)
    one = (~rot).astype(np.float32)[None, :]
    gblk = 2 * LANES
    gi = np.arange(gblk) // DIFF_DK
    gmean = (gi[:, None] == gi[None, :]).astype(np.float32) / DIFF_DK
    return dict(
        norm_w=norm_w[None, :],
        w_qkg=jnp.concatenate([w_in[:, :2 * D_MODEL], w_in[:, 3 * D_MODEL:]], axis=1).astype(BF16),
        w_vt=w_in[:, 2 * D_MODEL:3 * D_MODEL].T.astype(BF16),
        q_norm_w=jnp.tile(q_norm_w, maps)[None, :], k_norm_w=jnp.tile(k_norm_w, maps)[None, :],
        invf=inv.astype(F32)[:, None], sgn=jnp.asarray(sgn),
        sel=jnp.asarray(np.concatenate([sel, sel]), BF16), one=jnp.asarray(one), gmean=jnp.asarray(gmean, BF16),
        lq1=lq1[None, :], lk1=lk1[None, :], lq2=lq2[None, :], lk2=lk2[None, :],
        subln_w=subln_w[None, :], w_out=w_out.astype(BF16),
    )


def kernel(x, positions, a_norm_w, a_w_in, a_conv_w, a_conv_b, a_dt_bias, a_a_log, a_d_skip,
           a_ssd_norm_w, a_dw_w, a_dw_b, a_ln_w, a_ln_b, a_w_out, c_norm_w, c_w_in, c_q_norm_w,
           c_k_norm_w, c_lq1, c_lk1, c_lq2, c_lk2, c_subln_w, c_w_out):
    bsz, seq, d = x.shape
    assert d == D_MODEL and seq % max(MIX_ROWS, ATT_TQ, PROJ_ROWS) == 0
    assert a_norm_w.shape[0] == 1 and c_norm_w.shape[0] == 1
    m = bsz * seq
    x2 = x.reshape(m, d)

    pe = _even_params(a_norm_w[0], a_w_in[0], a_conv_w[0], a_conv_b[0], a_dt_bias[0], a_a_log[0],
                      a_d_skip[0], a_ssd_norm_w[0], a_dw_w[0], a_dw_b[0], a_ln_w[0], a_ln_b[0], a_w_out[0])
    proj, dt_raw = _even_inproj(x2, pe["norm_w"], pe["w_main"], pe["w_dt"])
    y = _even_mix(proj, dt_raw, pe, bsz, seq)
    x2 = _outproj(y, pe["w_out"], x2, "even_outproj")

    po = _odd_params(c_norm_w[0], c_w_in[0], c_q_norm_w[0], c_k_norm_w[0], c_lq1[0], c_lk1[0],
                     c_lq2[0], c_lk2[0], c_subln_w[0], c_w_out[0])
    q, k, vt, gate = _odd_inproj(x2, positions.reshape(m, 1), po, bsz, seq)
    shp = (bsz, seq, d)
    o = _attention(q.reshape(shp), k.reshape(shp), vt, gate.reshape(shp), po)
    x2 = _outproj(o.reshape(m, d), po["w_out"], x2, "odd_outproj")
    return x2.reshape(bsz, seq, d)
```

```python
import functools
import math

import jax
import jax.numpy as jnp
import numpy as np
from jax import lax
from jax.experimental import pallas as pl
from jax.experimental.pallas import tpu as pltpu

F32 = jnp.float32
BF16 = jnp.bfloat16

D_MODEL = 1024
CHUNK = 64
SSD_HEADS = 16
SSD_HEADDIM = 64
SSD_GROUPS = 4
HEADS_PER_GROUP = SSD_HEADS // SSD_GROUPS
D_STATE = 128
SSD_CONV = 4
CONF_KERNEL = 31
DIFF_HEADS = 8
DIFF_DK = 64
DIFF_DV = 128
ROT_DIM = 16
ROPE_THETA = 500000.0
EPS = 1e-6
LN_EPS = 1e-5
ODD_LAYER_IDX = 1
LAM_INIT = 0.8 - 0.6 * math.exp(-0.3 * ODD_LAYER_IDX)

LANES = 128
SUBLANES = 8
VMEM_LIMIT_BYTES = 56 * 1024 * 1024

PROJ_ROWS = 512
MIX_ROWS = 256
PAIR = 2 * CHUNK
XCARRY = 8
UCARRY = 32
CONV_ROWS = 128
ATT_TQ = 512
ATT_TK = 512
ATT_UNROLL = 4
NEG = -1e30


def _dot(a, b):
    return jnp.dot(a, b, preferred_element_type=F32)


def _dot_nt(a, b):
    return lax.dot_general(a, b, (((1,), (1,)), ((), ())), preferred_element_type=F32)


def _split2(x):
    hi = x.astype(BF16)
    lo = (x - hi.astype(F32)).astype(BF16)
    return hi, lo


def _expand(x, e2_ref):
    hi, lo = _split2(x)
    return _dot(jnp.concatenate([hi, lo], axis=1), e2_ref[...])


def _left_apply(m_ref, x):
    hi, lo = _split2(x)
    n = x.shape[1]
    r = _dot(m_ref[...], jnp.concatenate([hi, lo], axis=1))
    return r[:, :n] + r[:, n:]


def _rms_rows(x, w):
    ms = jnp.mean(x * x, axis=-1, keepdims=True)
    return x * lax.rsqrt(ms + EPS) * w


def _even_inproj_kernel(x_ref, nw_ref, w_ref, wdt_ref, proj_ref, dt_ref):
    hn = _rms_rows(x_ref[...], nw_ref[...]).astype(BF16)
    n_out = proj_ref.shape[1]
    for c in range(0, n_out, D_MODEL):
        proj_ref[:, c:c + D_MODEL] = _dot(hn, w_ref[:, c:c + D_MODEL]).astype(BF16)
    dt_ref[...] = _dot(hn, wdt_ref[...])


def _even_inproj(x2, nw, w, wdt):
    m = x2.shape[0]
    n_out = w.shape[1]
    tm = PROJ_ROWS
    return pl.pallas_call(
        _even_inproj_kernel,
        grid=(m // tm,),
        in_specs=[
            pl.BlockSpec((tm, D_MODEL), lambda i: (i, 0)),
            pl.BlockSpec((1, D_MODEL), lambda i: (0, 0)),
            pl.BlockSpec((D_MODEL, n_out), lambda i: (0, 0), pipeline_mode=pl.Buffered(1)),
            pl.BlockSpec((D_MODEL, LANES), lambda i: (0, 0), pipeline_mode=pl.Buffered(1)),
        ],
        out_specs=[
            pl.BlockSpec((tm, n_out), lambda i: (i, 0)),
            pl.BlockSpec((tm, LANES), lambda i: (i, 0)),
        ],
        out_shape=[
            jax.ShapeDtypeStruct((m, n_out), BF16),
            jax.ShapeDtypeStruct((m, LANES), F32),
        ],
        compiler_params=pltpu.CompilerParams(
            dimension_semantics=("arbitrary",), vmem_limit_bytes=VMEM_LIMIT_BYTES),
        name="even_inproj",
    )(x2, nw, w, wdt)


def _even_mix_kernel(xbc_ref, za_ref, gv_ref, gg_ref, zb_ref, dt_ref,
                     cw_ref, cb_ref, dtb_ref, alog_ref, dskip_ref, nrm_ref,
                     dww_ref, dwb_ref, lnw_ref, lnb_ref,
                     e64_ref, e128_ref, tri_ref, blk_ref,
                     out_ref,
                     xbuf, ubuf, hs, ybuf, uph, cbuf):
    t = pl.program_id(1)
    rows = za_ref.shape[0]
    d_ssd = SSD_HEADS * SSD_HEADDIM
    gw = HEADS_PER_GROUP * SSD_HEADDIM

    @pl.when(t == 0)
    def _():
        xbuf[0:XCARRY, :] = jnp.zeros((XCARRY, xbuf.shape[1]), F32)
        ubuf[0:UCARRY, :] = jnp.zeros((UCARRY, ubuf.shape[1]), F32)
        hs[...] = jnp.zeros(hs.shape, F32)

    xbuf[XCARRY:XCARRY + rows, :] = xbc_ref[...].astype(F32)
    acc = cb_ref[...]
    for k in range(SSD_CONV):
        off = XCARRY - (SSD_CONV - 1) + k
        acc = acc + cw_ref[k:k + 1, :] * xbuf[off:off + rows, :]
    xbuf[0:XCARRY, :] = xbuf[rows:rows + XCARRY, :]
    xc = acc * jax.nn.sigmoid(acc)
    xs = xc[:, :d_ssd]
    bm = xc[:, d_ssd:d_ssd + SSD_GROUPS * D_STATE]
    cm = xc[:, d_ssd + SSD_GROUPS * D_STATE:]

    dtr = dt_ref[...] + dtb_ref[...]
    dt = jnp.maximum(dtr, 0.0) + jnp.log1p(jnp.exp(-jnp.abs(dtr)))
    da = dt * (-jnp.exp(alog_ref[...]))
    cs = _left_apply(tri_ref, da)
    tot = _left_apply(blk_ref, da)
    dt_x = _expand(dt, e64_ref)
    ecs_x = _expand(jnp.exp(cs), e64_ref)
    edst_x = _expand(jnp.exp(tot - cs), e64_ref)
    ecd_x = _expand(jnp.exp(tot), e64_ref)
    cs_col = _expand(cs, e128_ref)
    cs_t = cs.T

    xd = xs * dt_x
    xdd = xd * edst_x

    li = lax.broadcasted_iota(jnp.int32, (PAIR, PAIR), 0)
    si = lax.broadcasted_iota(jnp.int32, (PAIR, PAIR), 1)
    intra = (si <= li) & ((si // CHUNK) == (li // CHUNK))
    lane_g = lax.broadcasted_iota(jnp.int32, (PAIR, gw), 1)
    row_g = lax.broadcasted_iota(jnp.int32, (PAIR, gw), 0)

    for pr in range(rows // PAIR):
        r0 = pr * PAIR
        for g in range(SSD_GROUPS):
            c_g = cm[r0:r0 + PAIR, g * D_STATE:(g + 1) * D_STATE].astype(BF16)
            b_g = bm[r0:r0 + PAIR, g * D_STATE:(g + 1) * D_STATE]
            cb = _dot_nt(c_g, b_g.astype(BF16))
            b_gt = b_g.T.astype(BF16)
            xd_g = xd[r0:r0 + PAIR, g * gw:(g + 1) * gw]
            xdd_g = xdd[r0:r0 + PAIR, g * gw:(g + 1) * gw]
            m_parts = []
            w_parts = []
            for e in range(HEADS_PER_GROUP):
                h = g * HEADS_PER_GROUP + e
                seg = cs_col[r0:r0 + PAIR, h * LANES:(h + 1) * LANES] - cs_t[h:h + 1, r0:r0 + PAIR]
                decay = jnp.exp(jnp.where(intra, seg, NEG))
                m_parts.append((cb * decay).astype(BF16))
                in_head = (lane_g // SSD_HEADDIM) == e
                w_parts.append(jnp.where(in_head, xd_g, 0.0).astype(BF16))
            y_diag = _dot(jnp.concatenate(m_parts, axis=1), jnp.concatenate(w_parts, axis=0))
            for j in range(PAIR // CHUNK):
                rj = r0 + j * CHUNK
                h_prev = hs[g]
                y_off = _dot(c_g[j * CHUNK:(j + 1) * CHUNK], h_prev.astype(BF16))
                y_off = y_off * ecs_x[rj:rj + CHUNK, g * gw:(g + 1) * gw]
                ybuf[rj:rj + CHUNK, g * gw:(g + 1) * gw] = y_diag[j * CHUNK:(j + 1) * CHUNK] + y_off
                in_chunk = (row_g // CHUNK) == j
                st = _dot(b_gt, jnp.where(in_chunk, xdd_g, 0.0).astype(BF16))
                hs[g] = h_prev * ecd_x[rj:rj + 1, g * gw:(g + 1) * gw] + st

    y = ybuf[...] + xs * dskip_ref[...]
    za = za_ref[...].astype(F32)
    y = y * (za * jax.nn.sigmoid(za))
    for g in range(SSD_GROUPS):
        yg = y[:, g * gw:(g + 1) * gw]
        out_ref[:, g * gw:(g + 1) * gw] = _rms_rows(yg, nrm_ref[:, g * gw:(g + 1) * gw]).astype(BF16)

    gg = gg_ref[...].astype(F32)
    ubuf[UCARRY:UCARRY + rows, :] = gv_ref[...].astype(F32) * jax.nn.sigmoid(gg)
    base = UCARRY - (CONF_KERNEL - 1)
    ntile = D_MODEL // LANES
    for sh in range(SUBLANES):
        span = UCARRY + rows - (SUBLANES if sh else 0)
        for c in range(ntile):
            uph[sh, c, 0:span, :] = ubuf[sh:sh + span, c * LANES:(c + 1) * LANES]
    ubuf[0:UCARRY, :] = ubuf[rows:rows + UCARRY, :]
    nblk = rows // CONV_ROWS

    def conv_block(i, carry):
        c = i // nblk
        r0 = pl.multiple_of((i % nblk) * CONV_ROWS, CONV_ROWS)
        nsub = CONV_ROWS // SUBLANES
        a = jnp.broadcast_to(dwb_ref[c], (nsub, SUBLANES, LANES))
        for k in range(CONF_KERNEL):
            sh = (base + k) % SUBLANES
            start = pl.multiple_of(r0 + (base + k - sh), SUBLANES)
            blk = uph[sh, c, pl.ds(start, CONV_ROWS), :].reshape(nsub, SUBLANES, LANES)
            a = a + dww_ref[k, c] * blk
        cbuf[c, pl.ds(r0, CONV_ROWS), :] = a.reshape(CONV_ROWS, LANES)
        return carry

    lax.fori_loop(0, ntile * nblk, conv_block, 0)
    acc_u = jnp.concatenate([cbuf[c] for c in range(ntile)], axis=1)
    mu = jnp.mean(acc_u, axis=-1, keepdims=True)
    cen = acc_u - mu
    var = jnp.mean(cen * cen, axis=-1, keepdims=True)
    un = cen * lax.rsqrt(var + LN_EPS) * lnw_ref[...] + lnb_ref[...]
    un = un * jax.nn.sigmoid(un)
    zb = zb_ref[...].astype(F32)
    out_ref[:, d_ssd:] = (un * (zb * jax.nn.sigmoid(zb))).astype(BF16)


def _even_mix(proj, dt_raw, p, bsz, seq):
    rows = MIX_ROWS
    nt = seq // rows
    d_ssd = SSD_HEADS * SSD_HEADDIM
    xbc_dim = d_ssd + 2 * SSD_GROUPS * D_STATE
    m = bsz * seq

    def rowblk(col):
        return lambda b, t: (b * nt + t, col)

    def const(shape):
        return pl.BlockSpec(shape, lambda b, t: (0, 0))

    in_specs = [
        pl.BlockSpec((rows, xbc_dim), rowblk(0)),
        pl.BlockSpec((rows, D_MODEL), rowblk(2)),
        pl.BlockSpec((rows, D_MODEL), rowblk(3)),
        pl.BlockSpec((rows, D_MODEL), rowblk(4)),
        pl.BlockSpec((rows, D_MODEL), rowblk(5)),
        pl.BlockSpec((rows, LANES), rowblk(0)),
        const((SSD_CONV, xbc_dim)), const((1, xbc_dim)),
        const((1, LANES)), const((1, LANES)),
        const((1, d_ssd)), const((1, d_ssd)),
        pl.BlockSpec((CONF_KERNEL, D_MODEL // LANES, SUBLANES, LANES), lambda b, t: (0, 0, 0, 0)),
        pl.BlockSpec((D_MODEL // LANES, SUBLANES, LANES), lambda b, t: (0, 0, 0)),
        const((1, D_MODEL)), const((1, D_MODEL)),
        const((2 * LANES, d_ssd)), const((2 * LANES, SSD_HEADS * LANES)),
        const((rows, rows)), const((rows, rows)),
    ]
    return pl.pallas_call(
        _even_mix_kernel,
        grid=(bsz, nt),
        in_specs=in_specs,
        out_specs=pl.BlockSpec((rows, 2 * D_MODEL), rowblk(0)),
        out_shape=jax.ShapeDtypeStruct((m, 2 * D_MODEL), BF16),
        scratch_shapes=[
            pltpu.VMEM((XCARRY + rows, xbc_dim), F32),
            pltpu.VMEM((UCARRY + rows, D_MODEL), F32),
            pltpu.VMEM((SSD_GROUPS, D_STATE, HEADS_PER_GROUP * SSD_HEADDIM), F32),
            pltpu.VMEM((rows, d_ssd), F32),
            pltpu.VMEM((SUBLANES, D_MODEL // LANES, UCARRY + rows, LANES), F32),
            pltpu.VMEM((D_MODEL // LANES, rows, LANES), F32),
        ],
        compiler_params=pltpu.CompilerParams(
            dimension_semantics=("arbitrary", "arbitrary"), vmem_limit_bytes=VMEM_LIMIT_BYTES),
        name="even_mix",
    )(proj, proj, proj, proj, proj, dt_raw,
      p["conv_w"], p["conv_b"], p["dt_bias"], p["a_log"], p["d_skip"], p["ssd_norm_w"],
      p["dw_w"], p["dw_b"], p["ln_w"], p["ln_b"],
      p["e64"], p["e128"], p["tri"], p["blk"])


def _outproj_kernel(y_ref, w_ref, x_ref, o_ref):
    o_ref[...] = x_ref[...] + _dot(y_ref[...], w_ref[...])


def _outproj(y, w, x2, name):
    m, kdim = y.shape
    tm = PROJ_ROWS
    return pl.pallas_call(
        _outproj_kernel,
        grid=(m // tm,),
        in_specs=[
            pl.BlockSpec((tm, kdim), lambda i: (i, 0)),
            pl.BlockSpec((kdim, D_MODEL), lambda i: (0, 0), pipeline_mode=pl.Buffered(1)),
            pl.BlockSpec((tm, D_MODEL), lambda i: (i, 0)),
        ],
        out_specs=pl.BlockSpec((tm, D_MODEL), lambda i: (i, 0)),
        out_shape=jax.ShapeDtypeStruct((m, D_MODEL), F32),
        compiler_params=pltpu.CompilerParams(
            dimension_semantics=("arbitrary",), vmem_limit_bytes=VMEM_LIMIT_BYTES),
        name=name,
    )(y, w, x2)


def _odd_inproj_kernel(x_ref, pos_ref, nw_ref, w_ref, wvt_ref, qnw_ref, knw_ref, invf_ref, sgn_ref,
                       sel_ref, one_ref, gmean_ref, q_ref, k_ref, vt_ref, g_ref):
    hn = _rms_rows(x_ref[...], nw_ref[...]).astype(BF16)
    d = D_MODEL
    half = ROT_DIM // 2
    ang_t = invf_ref[...] * pos_ref[0].astype(F32)
    pad = jnp.zeros((LANES - half, ang_t.shape[1]), F32)
    cos_c = jnp.concatenate([jnp.cos(ang_t), pad], axis=0).T
    sin_c = jnp.concatenate([jnp.sin(ang_t), pad], axis=0).T
    reps = d // LANES
    cos = jnp.concatenate([_expand(cos_c, sel_ref) + one_ref[...]] * reps, axis=1)
    sin = jnp.concatenate([_expand(sin_c, sel_ref) * sgn_ref[...]] * reps, axis=1)
    lane = lax.broadcasted_iota(jnp.int32, (1, d), 1) % DIFF_DK
    first_half = lane < (ROT_DIM // 2)
    gblk = gmean_ref.shape[0]

    def norm_rot(z, w_row, scale):
        sq = (z * z).astype(BF16)
        parts = []
        for c in range(0, d, gblk):
            parts.append(_dot(sq[:, c:c + gblk], gmean_ref[...]))
        ms = jnp.concatenate(parts, axis=1)
        zn = z * lax.rsqrt(ms + EPS) * w_row
        swapped = jnp.where(first_half, pltpu.roll(zn, d - ROT_DIM // 2, 1), pltpu.roll(zn, ROT_DIM // 2, 1))
        out = zn * cos + swapped * sin
        if scale != 1.0:
            out = out * scale
        return out.astype(BF16)

    q = _dot(hn, w_ref[:, 0:d])
    k = _dot(hn, w_ref[:, d:2 * d])
    q_ref[...] = norm_rot(q, qnw_ref[...], math.log2(math.e) * DIFF_DK ** -0.5)
    vt_ref[0] = _dot_nt(wvt_ref[...], hn).astype(BF16)
    k_ref[...] = norm_rot(k, knw_ref[...], 1.0)
    g_ref[...] = _dot(hn, w_ref[:, 2 * d:3 * d]).astype(BF16)


def _odd_inproj(x2, pos2, p, bsz, seq):
    m = x2.shape[0]
    tm = PROJ_ROWS
    nt = seq // tm
    row = lambda b, i: (b * nt + i, 0)
    const = lambda b, i: (0, 0)
    gblk = p["gmean"].shape[0]
    out_sds = jax.ShapeDtypeStruct((m, D_MODEL), BF16)
    return pl.pallas_call(
        _odd_inproj_kernel,
        grid=(bsz, nt),
        in_specs=[
            pl.BlockSpec((tm, D_MODEL), row),
            pl.BlockSpec((1, 1, tm), lambda b, i: (b * nt + i, 0, 0)),
            pl.BlockSpec((1, D_MODEL), const),
            pl.BlockSpec((D_MODEL, 3 * D_MODEL), const, pipeline_mode=pl.Buffered(1)),
            pl.BlockSpec((D_MODEL, D_MODEL), const, pipeline_mode=pl.Buffered(1)),
            pl.BlockSpec((1, D_MODEL), const),
            pl.BlockSpec((1, D_MODEL), const),
            pl.BlockSpec((ROT_DIM // 2, 1), const),
            pl.BlockSpec((1, LANES), const),
            pl.BlockSpec((2 * LANES, LANES), const),
            pl.BlockSpec((1, LANES), const),
            pl.BlockSpec((gblk, gblk), const),
        ],
        out_specs=[pl.BlockSpec((tm, D_MODEL), row), pl.BlockSpec((tm, D_MODEL), row),
                   pl.BlockSpec((1, D_MODEL, tm), lambda b, i: (b, 0, i)),
                   pl.BlockSpec((tm, D_MODEL), row)],
        out_shape=[out_sds, out_sds, jax.ShapeDtypeStruct((bsz, D_MODEL, seq), BF16), out_sds],
        compiler_params=pltpu.CompilerParams(
            dimension_semantics=("arbitrary", "arbitrary"), vmem_limit_bytes=VMEM_LIMIT_BYTES),
        name="odd_inproj",
    )(x2, pos2.reshape(m // tm, 1, tm), p["norm_w"], p["w_qkg"], p["w_vt"], p["q_norm_w"], p["k_norm_w"],
      p["invf"], p["sgn"], p["sel"], p["one"], p["gmean"])


def _attn_kernel(q_ref, k_ref, vt_ref, gate_ref, lq1_ref, lk1_ref, lq2_ref, lk2_ref, sub_ref,
                 o_ref, acc_sc, s_a, s_b, tmax_sc, ml_sc):
    qi = pl.program_id(2)
    tq = q_ref.shape[1]
    tk = ATT_TK
    assert tq == tk
    q = q_ref[0]
    lane = lax.broadcasted_iota(jnp.int32, q.shape, 1)
    zero = jnp.zeros_like(q)
    q_maps = (jnp.where(lane < DIFF_DK, q, zero), jnp.where(lane >= DIFF_DK, q, zero))

    acc_sc[...] = jnp.zeros(acc_sc.shape, F32)
    ml_sc[0] = jnp.full(ml_sc.shape[1:], NEG, F32)
    ml_sc[1] = jnp.zeros(ml_sc.shape[1:], F32)

    def tile_start(u):
        return pl.multiple_of(jnp.where(u == 0, qi, u - 1) * tk, tk)

    hq = tq // 2

    def scores_part(u, s_buf, slot, masked, mp, c):
        kj = k_ref[0, pl.ds(tile_start(u), tk), :]
        s = _dot_nt(kj, q_maps[mp][c:c + hq])
        if masked:
            key = lax.broadcasted_iota(jnp.int32, (tk, hq), 0)
            qry = lax.broadcasted_iota(jnp.int32, (tk, hq), 1) + c
            s = jnp.where((key // CHUNK) <= (qry // CHUNK), s, NEG)
        s_buf[mp, :, c:c + hq] = s
        tmax_sc[slot, mp, :, c:c + hq] = jnp.max(s, axis=0, keepdims=True)

    def scores(u, s_buf, slot, masked):
        for mp in range(2):
            for c in (0, hq):
                scores_part(u, s_buf, slot, masked, mp, c)

    def absorb_part(u, s_buf, slot, mp, c):
        vtj = vt_ref[0, :, pl.ds(tile_start(u), tk)]
        m_prev = ml_sc[0, mp, :, c:c + hq]
        m_new = jnp.maximum(m_prev, tmax_sc[slot, mp, :, c:c + hq])
        p = jnp.exp2(s_buf[mp, :, c:c + hq] - m_new)
        alpha = jnp.exp2(m_prev - m_new)
        ml_sc[1, mp, :, c:c + hq] = alpha * ml_sc[1, mp, :, c:c + hq] + jnp.sum(p, axis=0, keepdims=True)
        acc_sc[mp, :, c:c + hq] = alpha * acc_sc[mp, :, c:c + hq] + _dot(vtj, p.astype(BF16))
        ml_sc[0, mp, :, c:c + hq] = m_new

    def absorb(u, s_buf, slot):
        for mp in range(2):
            for c in (0, hq):
                absorb_part(u, s_buf, slot, mp, c)

    scores(0, s_a, 0, True)

    bufs = ((s_a, 0), (s_b, 1))
    assert ATT_UNROLL % 2 == 0

    def steps(u0, count, tail):
        for j in range(count):
            nxt, cur = bufs[(j + 1) % 2], bufs[j % 2]
            for mp in range(2):
                for c in (0, hq):
                    scores_part(u0 + j + 1, nxt[0], nxt[1], False, mp, c)
                    absorb_part(u0 + j, cur[0], cur[1], mp, c)
        if tail:
            last = bufs[count % 2]
            absorb(u0 + count, last[0], last[1])

    def body(t, carry):
        steps(ATT_UNROLL * t, ATT_UNROLL, False)
        return carry

    lax.fori_loop(0, qi // ATT_UNROLL, body, 0)
    done = (qi // ATT_UNROLL) * ATT_UNROLL
    for r in range(ATT_UNROLL):
        @pl.when(qi - done == r)
        def _(r=r):
            steps(done, r, True)

    lam = (jnp.exp(jnp.sum(lq1_ref[...] * lk1_ref[...], axis=-1, keepdims=True))
           - jnp.exp(jnp.sum(lq2_ref[...] * lk2_ref[...], axis=-1, keepdims=True)) + LAM_INIT)
    o_t = acc_sc[0] / ml_sc[1, 0] - lam * (acc_sc[1] / ml_sc[1, 1])
    o = _rms_rows(o_t.T, sub_ref[...]) * (1.0 - LAM_INIT)
    g = gate_ref[0].astype(F32)
    o_ref[0] = (o * (g * jax.nn.sigmoid(g))).astype(BF16)


def _attention(q, k, vt, gate, p):
    bsz, seq, _ = q.shape
    tq = ATT_TQ
    qblk = pl.BlockSpec((1, tq, DIFF_DV), lambda b, h, i: (b, i, h))
    kblk = pl.BlockSpec((1, seq, DIFF_DV), lambda b, h, i: (b, 0, h))
    vtblk = pl.BlockSpec((1, DIFF_DV, seq), lambda b, h, i: (b, h, 0))
    vec = pl.BlockSpec((1, DIFF_DK), lambda b, h, i: (0, 0))
    return pl.pallas_call(
        _attn_kernel,
        grid=(bsz, DIFF_HEADS, seq // tq),
        in_specs=[qblk, kblk, vtblk, qblk, vec, vec, vec, vec,
                  pl.BlockSpec((1, DIFF_DV), lambda b, h, i: (0, 0))],
        out_specs=qblk,
        out_shape=jax.ShapeDtypeStruct((bsz, seq, DIFF_HEADS * DIFF_DV), BF16),
        scratch_shapes=[
            pltpu.VMEM((2, DIFF_DV, tq), F32),
            pltpu.VMEM((2, ATT_TK, tq), F32),
            pltpu.VMEM((2, ATT_TK, tq), F32),
            pltpu.VMEM((2, 2, 1, tq), F32),
            pltpu.VMEM((2, 2, 1, tq), F32),
        ],
        compiler_params=pltpu.CompilerParams(
            dimension_semantics=("arbitrary", "arbitrary", "arbitrary"),
            vmem_limit_bytes=VMEM_LIMIT_BYTES),
        name="diff_attn",
    )(q, k, vt, gate, p["lq1"], p["lk1"], p["lq2"], p["lk2"], p["subln_w"])


def _even_params(norm_w, w_in, conv_w, conv_b, dt_bias, a_log, d_skip, ssd_norm_w,
                 dw_w, dw_b, ln_w, ln_b, w_out):
    d_ssd = SSD_HEADS * SSD_HEADDIM
    xbc_dim = d_ssd + 2 * SSD_GROUPS * D_STATE
    o_za, o_xbc, o_dt = 0, d_ssd, d_ssd + xbc_dim
    o_gv = o_dt + SSD_HEADS
    o_gg, o_zb = o_gv + D_MODEL, o_gv + 2 * D_MODEL
    w_main = jnp.concatenate([
        w_in[:, o_xbc:o_xbc + xbc_dim], w_in[:, o_za:o_za + d_ssd],
        w_in[:, o_gv:o_gv + D_MODEL], w_in[:, o_gg:o_gg + D_MODEL], w_in[:, o_zb:o_zb + D_MODEL],
    ], axis=1).astype(BF16)
    pad = LANES - SSD_HEADS
    w_dt = jnp.pad(w_in[:, o_dt:o_dt + SSD_HEADS], ((0, 0), (0, pad))).astype(BF16)

    heads = np.arange(LANES)[:, None]
    e64 = (heads == (np.arange(d_ssd)[None, :] // SSD_HEADDIM)).astype(np.float32)
    e128 = (heads == (np.arange(SSD_HEADS * LANES)[None, :] // LANES)).astype(np.float32)
    r = np.arange(MIX_ROWS)
    same = (r[:, None] // CHUNK) == (r[None, :] // CHUNK)
    tri = (same & (r[None, :] <= r[:, None])).astype(np.float32)
    return dict(
        norm_w=norm_w[None, :], w_main=w_main, w_dt=w_dt,
        conv_w=conv_w, conv_b=conv_b[None, :],
        dt_bias=jnp.pad(dt_bias, (0, pad))[None, :], a_log=jnp.pad(a_log, (0, pad))[None, :],
        d_skip=jnp.repeat(d_skip, SSD_HEADDIM)[None, :], ssd_norm_w=ssd_norm_w[None, :],
        dw_w=jnp.broadcast_to(dw_w.reshape(CONF_KERNEL, D_MODEL // LANES, 1, LANES),
                              (CONF_KERNEL, D_MODEL // LANES, SUBLANES, LANES)),
        dw_b=jnp.broadcast_to(dw_b.reshape(D_MODEL // LANES, 1, LANES), (D_MODEL // LANES, SUBLANES, LANES)),
        ln_w=ln_w[None, :], ln_b=ln_b[None, :],
        e64=jnp.asarray(np.concatenate([e64, e64]), BF16), e128=jnp.asarray(np.concatenate([e128, e128]), BF16),
        tri=jnp.asarray(tri, BF16), blk=jnp.asarray(same.astype(np.float32), BF16),
        w_out=w_out.astype(BF16),
    )


def _odd_params(norm_w, w_in, q_norm_w, k_norm_w, lq1, lk1, lq2, lk2, subln_w, w_out):
    maps = D_MODEL // DIFF_DK
    lane = np.arange(LANES) % DIFF_DK
    half = ROT_DIM // 2
    inv = (ROPE_THETA ** (-2.0 * jnp.arange(half, dtype=F32) / ROT_DIM))
    rot = lane < ROT_DIM
    sgn = np.where(lane < half, -1.0, 1.0).astype(np.float32)[None, :]
    sel = ((np.arange(LANES)[:, None] == (lane % half)[None, :]) & rot[None, :]).astype(np.float32)
    one = (~rot).astype(np.float32)[None, :]
    gblk = 2 * LANES
    gi = np.arange(gblk) // DIFF_DK
    gmean = (gi[:, None] == gi[None, :]).astype(np.float32) / DIFF_DK
    return dict(
        norm_w=norm_w[None, :],
        w_qkg=jnp.concatenate([w_in[:, :2 * D_MODEL], w_in[:, 3 * D_MODEL:]], axis=1).astype(BF16),
        w_vt=w_in[:, 2 * D_MODEL:3 * D_MODEL].T.astype(BF16),
        q_norm_w=jnp.tile(q_norm_w, maps)[None, :], k_norm_w=jnp.tile(k_norm_w, maps)[None, :],
        invf=inv.astype(F32)[:, None], sgn=jnp.asarray(sgn),
        sel=jnp.asarray(np.concatenate([sel, sel]), BF16), one=jnp.asarray(one), gmean=jnp.asarray(gmean, BF16),
        lq1=lq1[None, :], lk1=lk1[None, :], lq2=lq2[None, :], lk2=lk2[None, :],
        subln_w=subln_w[None, :], w_out=w_out.astype(BF16),
    )


def kernel(x, positions, a_norm_w, a_w_in, a_conv_w, a_conv_b, a_dt_bias, a_a_log, a_d_skip,
           a_ssd_norm_w, a_dw_w, a_dw_b, a_ln_w, a_ln_b, a_w_out, c_norm_w, c_w_in, c_q_norm_w,
           c_k_norm_w, c_lq1, c_lk1, c_lq2, c_lk2, c_subln_w, c_w_out):
    bsz, seq, d = x.shape
    assert d == D_MODEL and seq % max(MIX_ROWS, ATT_TQ, PROJ_ROWS) == 0
    assert a_norm_w.shape[0] == 1 and c_norm_w.shape[0] == 1
    m = bsz * seq
    x2 = x.reshape(m, d)

    pe = _even_params(a_norm_w[0], a_w_in[0], a_conv_w[0], a_conv_b[0], a_dt_bias[0], a_a_log[0],
                      a_d_skip[0], a_ssd_norm_w[0], a_dw_w[0], a_dw_b[0], a_ln_w[0], a_ln_b[0], a_w_out[0])
    proj, dt_raw = _even_inproj(x2, pe["norm_w"], pe["w_main"], pe["w_dt"])
    y = _even_mix(proj, dt_raw, pe, bsz, seq)
    x2 = _outproj(y, pe["w_out"], x2, "even_outproj")

    po = _odd_params(c_norm_w[0], c_w_in[0], c_q_norm_w[0], c_k_norm_w[0], c_lq1[0], c_lk1[0],
                     c_lq2[0], c_lk2[0], c_subln_w[0], c_w_out[0])
    q, k, vt, gate = _odd_inproj(x2, positions.reshape(m, 1), po, bsz, seq)
    shp = (bsz, seq, d)
    o = _attention(q.reshape(shp), k.reshape(shp), vt, gate.reshape(shp), po)
    x2 = _outproj(o.reshape(m, d), po["w_out"], x2, "odd_outproj")
    return x2.reshape(bsz, seq, d)
```

```python
import functools
import math

import jax
import jax.numpy as jnp
import numpy as np
from jax import lax
from jax.experimental import pallas as pl
from jax.experimental.pallas import tpu as pltpu

F32 = jnp.float32
BF16 = jnp.bfloat16

D_MODEL = 1024
CHUNK = 64
SSD_HEADS = 16
SSD_HEADDIM = 64
SSD_GROUPS = 4
HEADS_PER_GROUP = SSD_HEADS // SSD_GROUPS
D_STATE = 128
SSD_CONV = 4
CONF_KERNEL = 31
DIFF_HEADS = 8
DIFF_DK = 64
DIFF_DV = 128
ROT_DIM = 16
ROPE_THETA = 500000.0
EPS = 1e-6
LN_EPS = 1e-5
ODD_LAYER_IDX = 1
LAM_INIT = 0.8 - 0.6 * math.exp(-0.3 * ODD_LAYER_IDX)

LANES = 128
SUBLANES = 8
VMEM_LIMIT_BYTES = 56 * 1024 * 1024

PROJ_ROWS = 512
MIX_ROWS = 256
PAIR = 2 * CHUNK
XCARRY = 8
UCARRY = 32
CONV_ROWS = 128
ATT_TQ = 512
ATT_TK = 512
ATT_UNROLL = 4
NEG = -1e30


def _dot(a, b):
    return jnp.dot(a, b, preferred_element_type=F32)


def _dot_nt(a, b):
    return lax.dot_general(a, b, (((1,), (1,)), ((), ())), preferred_element_type=F32)


def _split2(x):
    hi = x.astype(BF16)
    lo = (x - hi.astype(F32)).astype(BF16)
    return hi, lo


def _expand(x, e2_ref):
    hi, lo = _split2(x)
    return _dot(jnp.concatenate([hi, lo], axis=1), e2_ref[...])


def _left_apply(m_ref, x):
    hi, lo = _split2(x)
    n = x.shape[1]
    r = _dot(m_ref[...], jnp.concatenate([hi, lo], axis=1))
    return r[:, :n] + r[:, n:]


def _rms_rows(x, w):
    ms = jnp.mean(x * x, axis=-1, keepdims=True)
    return x * lax.rsqrt(ms + EPS) * w


def _even_inproj_kernel(x_ref, nw_ref, w_ref, wdt_ref, proj_ref, dt_ref):
    hn = _rms_rows(x_ref[...], nw_ref[...]).astype(BF16)
    n_out = proj_ref.shape[1]
    for c in range(0, n_out, D_MODEL):
        proj_ref[:, c:c + D_MODEL] = _dot(hn, w_ref[:, c:c + D_MODEL]).astype(BF16)
    dt_ref[...] = _dot(hn, wdt_ref[...])


def _even_inproj(x2, nw, w, wdt):
    m = x2.shape[0]
    n_out = w.shape[1]
    tm = PROJ_ROWS
    return pl.pallas_call(
        _even_inproj_kernel,
        grid=(m // tm,),
        in_specs=[
            pl.BlockSpec((tm, D_MODEL), lambda i: (i, 0)),
            pl.BlockSpec((1, D_MODEL), lambda i: (0, 0)),
            pl.BlockSpec((D_MODEL, n_out), lambda i: (0, 0), pipeline_mode=pl.Buffered(1)),
            pl.BlockSpec((D_MODEL, LANES), lambda i: (0, 0), pipeline_mode=pl.Buffered(1)),
        ],
        out_specs=[
            pl.BlockSpec((tm, n_out), lambda i: (i, 0)),
            pl.BlockSpec((tm, LANES), lambda i: (i, 0)),
        ],
        out_shape=[
            jax.ShapeDtypeStruct((m, n_out), BF16),
            jax.ShapeDtypeStruct((m, LANES), F32),
        ],
        compiler_params=pltpu.CompilerParams(
            dimension_semantics=("arbitrary",), vmem_limit_bytes=VMEM_LIMIT_BYTES),
        name="even_inproj",
    )(x2, nw, w, wdt)


def _even_mix_kernel(xbc_ref, za_ref, gv_ref, gg_ref, zb_ref, dt_ref,
                     cw_ref, cb_ref, dtb_ref, alog_ref, dskip_ref, nrm_ref,
                     dww_ref, dwb_ref, lnw_ref, lnb_ref,
                     e64_ref, e128_ref, tri_ref, blk_ref,
                     out_ref,
                     xbuf, ubuf, hs, ybuf, uph, cbuf):
    t = pl.program_id(1)
    rows = za_ref.shape[0]
    d_ssd = SSD_HEADS * SSD_HEADDIM
    gw = HEADS_PER_GROUP * SSD_HEADDIM

    @pl.when(t == 0)
    def _():
        xbuf[0:XCARRY, :] = jnp.zeros((XCARRY, xbuf.shape[1]), F32)
        ubuf[0:UCARRY, :] = jnp.zeros((UCARRY, ubuf.shape[1]), F32)
        hs[...] = jnp.zeros(hs.shape, F32)

    xbuf[XCARRY:XCARRY + rows, :] = xbc_ref[...].astype(F32)
    acc = cb_ref[...]
    for k in range(SSD_CONV):
        off = XCARRY - (SSD_CONV - 1) + k
        acc = acc + cw_ref[k:k + 1, :] * xbuf[off:off + rows, :]
    xbuf[0:XCARRY, :] = xbuf[rows:rows + XCARRY, :]
    xc = acc * jax.nn.sigmoid(acc)
    xs = xc[:, :d_ssd]
    bm = xc[:, d_ssd:d_ssd + SSD_GROUPS * D_STATE]
    cm = xc[:, d_ssd + SSD_GROUPS * D_STATE:]

    dtr = dt_ref[...] + dtb_ref[...]
    dt = jnp.maximum(dtr, 0.0) + jnp.log1p(jnp.exp(-jnp.abs(dtr)))
    da = dt * (-jnp.exp(alog_ref[...]))
    cs = _left_apply(tri_ref, da)
    tot = _left_apply(blk_ref, da)
    dt_x = _expand(dt, e64_ref)
    ecs_x = _expand(jnp.exp(cs), e64_ref)
    edst_x = _expand(jnp.exp(tot - cs), e64_ref)
    ecd_x = _expand(jnp.exp(tot), e64_ref)
    cs_col = _expand(cs, e128_ref)
    cs_t = cs.T

    xd = xs * dt_x
    xdd = xd * edst_x

    li = lax.broadcasted_iota(jnp.int32, (PAIR, PAIR), 0)
    si = lax.broadcasted_iota(jnp.int32, (PAIR, PAIR), 1)
    intra = (si <= li) & ((si // CHUNK) == (li // CHUNK))
    lane_g = lax.broadcasted_iota(jnp.int32, (PAIR, gw), 1)
    row_g = lax.broadcasted_iota(jnp.int32, (PAIR, gw), 0)

    for pr in range(rows // PAIR):
        r0 = pr * PAIR
        for g in range(SSD_GROUPS):
            c_g = cm[r0:r0 + PAIR, g * D_STATE:(g + 1) * D_STATE].astype(BF16)
            b_g = bm[r0:r0 + PAIR, g * D_STATE:(g + 1) * D_STATE]
            cb = _dot_nt(c_g, b_g.astype(BF16))
            b_gt = b_g.T.astype(BF16)
            xd_g = xd[r0:r0 + PAIR, g * gw:(g + 1) * gw]
            xdd_g = xdd[r0:r0 + PAIR, g * gw:(g + 1) * gw]
            m_parts = []
            w_parts = []
            for e in range(HEADS_PER_GROUP):
                h = g * HEADS_PER_GROUP + e
                seg = cs_col[r0:r0 + PAIR, h * LANES:(h + 1) * LANES] - cs_t[h:h + 1, r0:r0 + PAIR]
                decay = jnp.exp(jnp.where(intra, seg, NEG))
                m_parts.append((cb * decay).astype(BF16))
                in_head = (lane_g // SSD_HEADDIM) == e
                w_parts.append(jnp.where(in_head, xd_g, 0.0).astype(BF16))
            y_diag = _dot(jnp.concatenate(m_parts, axis=1), jnp.concatenate(w_parts, axis=0))
            for j in range(PAIR // CHUNK):
                rj = r0 + j * CHUNK
                h_prev = hs[g]
                y_off = _dot(c_g[j * CHUNK:(j + 1) * CHUNK], h_prev.astype(BF16))
                y_off = y_off * ecs_x[rj:rj + CHUNK, g * gw:(g + 1) * gw]
                ybuf[rj:rj + CHUNK, g * gw:(g + 1) * gw] = y_diag[j * CHUNK:(j + 1) * CHUNK] + y_off
                in_chunk = (row_g // CHUNK) == j
                st = _dot(b_gt, jnp.where(in_chunk, xdd_g, 0.0).astype(BF16))
                hs[g] = h_prev * ecd_x[rj:rj + 1, g * gw:(g + 1) * gw] + st

    y = ybuf[...] + xs * dskip_ref[...]
    za = za_ref[...].astype(F32)
    y = y * (za * jax.nn.sigmoid(za))
    for g in range(SSD_GROUPS):
        yg = y[:, g * gw:(g + 1) * gw]
        out_ref[:, g * gw:(g + 1) * gw] = _rms_rows(yg, nrm_ref[:, g * gw:(g + 1) * gw]).astype(BF16)

    gg = gg_ref[...].astype(F32)
    ubuf[UCARRY:UCARRY + rows, :] = gv_ref[...].astype(F32) * jax.nn.sigmoid(gg)
    base = UCARRY - (CONF_KERNEL - 1)
    ntile = D_MODEL // LANES
    for sh in range(SUBLANES):
        span = UCARRY + rows - (SUBLANES if sh else 0)
        for c in range(ntile):
            uph[sh, c, 0:span, :] = ubuf[sh:sh + span, c * LANES:(c + 1) * LANES]
    ubuf[0:UCARRY, :] = ubuf[rows:rows + UCARRY, :]
    nblk = rows // CONV_ROWS

    def conv_block(i, carry):
        c = i // nblk
        r0 = pl.multiple_of((i % nblk) * CONV_ROWS, CONV_ROWS)
        nsub = CONV_ROWS // SUBLANES
        a = jnp.broadcast_to(dwb_ref[c], (nsub, SUBLANES, LANES))
        for k in range(CONF_KERNEL):
            sh = (base + k) % SUBLANES
            start = pl.multiple_of(r0 + (base + k - sh), SUBLANES)
            blk = uph[sh, c, pl.ds(start, CONV_ROWS), :].reshape(nsub, SUBLANES, LANES)
            a = a + dww_ref[k, c] * blk
        cbuf[c, pl.ds(r0, CONV_ROWS), :] = a.reshape(CONV_ROWS, LANES)
        return carry

    lax.fori_loop(0, ntile * nblk, conv_block, 0)
    acc_u = jnp.concatenate([cbuf[c] for c in range(ntile)], axis=1)
    mu = jnp.mean(acc_u, axis=-1, keepdims=True)
    cen = acc_u - mu
    var = jnp.mean(cen * cen, axis=-1, keepdims=True)
    un = cen * lax.rsqrt(var + LN_EPS) * lnw_ref[...] + lnb_ref[...]
    un = un * jax.nn.sigmoid(un)
    zb = zb_ref[...].astype(F32)
    out_ref[:, d_ssd:] = (un * (zb * jax.nn.sigmoid(zb))).astype(BF16)


def _even_mix(proj, dt_raw, p, bsz, seq):
    rows = MIX_ROWS
    nt = seq // rows
    d_ssd = SSD_HEADS * SSD_HEADDIM
    xbc_dim = d_ssd + 2 * SSD_GROUPS * D_STATE
    m = bsz * seq

    def rowblk(col):
        return lambda b, t: (b * nt + t, col)

    def const(shape):
        return pl.BlockSpec(shape, lambda b, t: (0, 0))

    in_specs = [
        pl.BlockSpec((rows, xbc_dim), rowblk(0)),
        pl.BlockSpec((rows, D_MODEL), rowblk(2)),
        pl.BlockSpec((rows, D_MODEL), rowblk(3)),
        pl.BlockSpec((rows, D_MODEL), rowblk(4)),
        pl.BlockSpec((rows, D_MODEL), rowblk(5)),
        pl.BlockSpec((rows, LANES), rowblk(0)),
        const((SSD_CONV, xbc_dim)), const((1, xbc_dim)),
        const((1, LANES)), const((1, LANES)),
        const((1, d_ssd)), const((1, d_ssd)),
        pl.BlockSpec((CONF_KERNEL, D_MODEL // LANES, SUBLANES, LANES), lambda b, t: (0, 0, 0, 0)),
        pl.BlockSpec((D_MODEL // LANES, SUBLANES, LANES), lambda b, t: (0, 0, 0)),
        const((1, D_MODEL)), const((1, D_MODEL)),
        const((2 * LANES, d_ssd)), const((2 * LANES, SSD_HEADS * LANES)),
        const((rows, rows)), const((rows, rows)),
    ]
    return pl.pallas_call(
        _even_mix_kernel,
        grid=(bsz, nt),
        in_specs=in_specs,
        out_specs=pl.BlockSpec((rows, 2 * D_MODEL), rowblk(0)),
        out_shape=jax.ShapeDtypeStruct((m, 2 * D_MODEL), BF16),
        scratch_shapes=[
            pltpu.VMEM((XCARRY + rows, xbc_dim), F32),
            pltpu.VMEM((UCARRY + rows, D_MODEL), F32),
            pltpu.VMEM((SSD_GROUPS, D_STATE, HEADS_PER_GROUP * SSD_HEADDIM), F32),
            pltpu.VMEM((rows, d_ssd), F32),
            pltpu.VMEM((SUBLANES, D_MODEL // LANES, UCARRY + rows, LANES), F32),
            pltpu.VMEM((D_MODEL // LANES, rows, LANES), F32),
        ],
        compiler_params=pltpu.CompilerParams(
            dimension_semantics=("arbitrary", "arbitrary"), vmem_limit_bytes=VMEM_LIMIT_BYTES),
        name="even_mix",
    )(proj, proj, proj, proj, proj, dt_raw,
      p["conv_w"], p["conv_b"], p["dt_bias"], p["a_log"], p["d_skip"], p["ssd_norm_w"],
      p["dw_w"], p["dw_b"], p["ln_w"], p["ln_b"],
      p["e64"], p["e128"], p["tri"], p["blk"])


def _outproj_kernel(y_ref, w_ref, x_ref, o_ref):
    o_ref[...] = x_ref[...] + _dot(y_ref[...], w_ref[...])


def _outproj(y, w, x2, name):
    m, kdim = y.shape
    tm = PROJ_ROWS
    return pl.pallas_call(
        _outproj_kernel,
        grid=(m // tm,),
        in_specs=[
            pl.BlockSpec((tm, kdim), lambda i: (i, 0)),
            pl.BlockSpec((kdim, D_MODEL), lambda i: (0, 0), pipeline_mode=pl.Buffered(1)),
            pl.BlockSpec((tm, D_MODEL), lambda i: (i, 0)),
        ],
        out_specs=pl.BlockSpec((tm, D_MODEL), lambda i: (i, 0)),
        out_shape=jax.ShapeDtypeStruct((m, D_MODEL), F32),
        compiler_params=pltpu.CompilerParams(
            dimension_semantics=("arbitrary",), vmem_limit_bytes=VMEM_LIMIT_BYTES),
        name=name,
    )(y, w, x2)


def _odd_inproj_kernel(x_ref, pos_ref, nw_ref, w_ref, wvt_ref, qnw_ref, knw_ref, invf_ref, sgn_ref,
                       sel_ref, one_ref, gmean_ref, q_ref, k_ref, vt_ref, g_ref):
    hn = _rms_rows(x_ref[...], nw_ref[...]).astype(BF16)
    d = D_MODEL
    half = ROT_DIM // 2
    ang_t = invf_ref[...] * pos_ref[0].astype(F32)
    pad = jnp.zeros((LANES - half, ang_t.shape[1]), F32)
    cos_c = jnp.concatenate([jnp.cos(ang_t), pad], axis=0).T
    sin_c = jnp.concatenate([jnp.sin(ang_t), pad], axis=0).T
    reps = d // LANES
    cos = jnp.concatenate([_expand(cos_c, sel_ref) + one_ref[...]] * reps, axis=1)
    sin = jnp.concatenate([_expand(sin_c, sel_ref) * sgn_ref[...]] * reps, axis=1)
    lane = lax.broadcasted_iota(jnp.int32, (1, d), 1) % DIFF_DK
    first_half = lane < (ROT_DIM // 2)
    gblk = gmean_ref.shape[0]

    def norm_rot(z, w_row, scale):
        sq = (z * z).astype(BF16)
        parts = []
        for c in range(0, d, gblk):
            parts.append(_dot(sq[:, c:c + gblk], gmean_ref[...]))
        ms = jnp.concatenate(parts, axis=1)
        zn = z * lax.rsqrt(ms + EPS) * w_row
        swapped = jnp.where(first_half, pltpu.roll(zn, d - ROT_DIM // 2, 1), pltpu.roll(zn, ROT_DIM // 2, 1))
        out = zn * cos + swapped * sin
        if scale != 1.0:
            out = out * scale
        return out.astype(BF16)

    q = _dot(hn, w_ref[:, 0:d])
    k = _dot(hn, w_ref[:, d:2 * d])
    q_ref[...] = norm_rot(q, qnw_ref[...], math.log2(math.e) * DIFF_DK ** -0.5)
    vt_ref[0] = _dot_nt(wvt_ref[...], hn).astype(BF16)
    k_ref[...] = norm_rot(k, knw_ref[...], 1.0)
    g_ref[...] = _dot(hn, w_ref[:, 2 * d:3 * d]).astype(BF16)


def _odd_inproj(x2, pos2, p, bsz, seq):
    m = x2.shape[0]
    tm = PROJ_ROWS
    nt = seq // tm
    row = lambda b, i: (b * nt + i, 0)
    const = lambda b, i: (0, 0)
    gblk = p["gmean"].shape[0]
    out_sds = jax.ShapeDtypeStruct((m, D_MODEL), BF16)
    return pl.pallas_call(
        _odd_inproj_kernel,
        grid=(bsz, nt),
        in_specs=[
            pl.BlockSpec((tm, D_MODEL), row),
            pl.BlockSpec((1, 1, tm), lambda b, i: (b * nt + i, 0, 0)),
            pl.BlockSpec((1, D_MODEL), const),
            pl.BlockSpec((D_MODEL, 3 * D_MODEL), const, pipeline_mode=pl.Buffered(1)),
            pl.BlockSpec((D_MODEL, D_MODEL), const, pipeline_mode=pl.Buffered(1)),
            pl.BlockSpec((1, D_MODEL), const),
            pl.BlockSpec((1, D_MODEL), const),
            pl.BlockSpec((ROT_DIM // 2, 1), const),
            pl.BlockSpec((1, LANES), const),
            pl.BlockSpec((2 * LANES, LANES), const),
            pl.BlockSpec((1, LANES), const),
            pl.BlockSpec((gblk, gblk), const),
        ],
        out_specs=[pl.BlockSpec((tm, D_MODEL), row), pl.BlockSpec((tm, D_MODEL), row),
                   pl.BlockSpec((1, D_MODEL, tm), lambda b, i: (b, 0, i)),
                   pl.BlockSpec((tm, D_MODEL), row)],
        out_shape=[out_sds, out_sds, jax.ShapeDtypeStruct((bsz, D_MODEL, seq), BF16), out_sds],
        compiler_params=pltpu.CompilerParams(
            dimension_semantics=("arbitrary", "arbitrary"), vmem_limit_bytes=VMEM_LIMIT_BYTES),
        name="odd_inproj",
    )(x2, pos2.reshape(m // tm, 1, tm), p["norm_w"], p["w_qkg"], p["w_vt"], p["q_norm_w"], p["k_norm_w"],
      p["invf"], p["sgn"], p["sel"], p["one"], p["gmean"])


def _attn_kernel(q_ref, k_ref, vt_ref, gate_ref, lq1_ref, lk1_ref, lq2_ref, lk2_ref, sub_ref,
                 o_ref, acc_sc, s_a, s_b, s_d, tmax_sc, ml_sc):
    qi = pl.program_id(2)
    nq = pl.num_programs(2)
    tq = o_ref.shape[1]
    tk = ATT_TK
    assert tq == tk and ATT_UNROLL % 2 == 0
    hq = tq // 2

    def query_maps(blk):
        q = q_ref[0, pl.ds(pl.multiple_of(blk * tq, tq), tq), :]
        lane = lax.broadcasted_iota(jnp.int32, q.shape, 1)
        zero = jnp.zeros_like(q)
        return (jnp.where(lane < DIFF_DK, q, zero), jnp.where(lane >= DIFF_DK, q, zero))

    q_maps = query_maps(qi)
    nxt_blk = jnp.minimum(qi + 1, nq - 1)
    q_next = query_maps(nxt_blk)

    acc_sc[...] = jnp.zeros(acc_sc.shape, F32)
    ml_sc[0] = jnp.full(ml_sc.shape[1:], NEG, F32)
    ml_sc[1] = jnp.zeros(ml_sc.shape[1:], F32)

    parts = [(mp, c) for mp in range(2) for c in (0, hq)]

    def scores_part(kblk, qm, buf, masked, mp, c):
        s_buf, slot = buf
        kj = k_ref[0, pl.ds(pl.multiple_of(kblk * tk, tk), tk), :]
        s = _dot_nt(kj, qm[mp][c:c + hq])
        if masked:
            key = lax.broadcasted_iota(jnp.int32, (tk, hq), 0)
            qry = lax.broadcasted_iota(jnp.int32, (tk, hq), 1) + c
            s = jnp.where((key // CHUNK) <= (qry // CHUNK), s, NEG)
        s_buf[mp, :, c:c + hq] = s
        tmax_sc[slot, mp, :, c:c + hq] = jnp.max(s, axis=0, keepdims=True)

    def absorb_part(kblk, buf, mp, c):
        s_buf, slot = buf
        vtj = vt_ref[0, :, pl.ds(pl.multiple_of(kblk * tk, tk), tk)]
        m_prev = ml_sc[0, mp, :, c:c + hq]
        m_new = jnp.maximum(m_prev, tmax_sc[slot, mp, :, c:c + hq])
        p = jnp.exp2(s_buf[mp, :, c:c + hq] - m_new)
        alpha = jnp.exp2(m_prev - m_new)
        ml_sc[1, mp, :, c:c + hq] = alpha * ml_sc[1, mp, :, c:c + hq] + jnp.sum(p, axis=0, keepdims=True)
        acc_sc[mp, :, c:c + hq] = alpha * acc_sc[mp, :, c:c + hq] + _dot(vtj, p.astype(BF16))
        ml_sc[0, mp, :, c:c + hq] = m_new

    def pair(score, absorb):
        for mp, c in parts:
            if score is not None:
                scores_part(*score, mp, c)
            if absorb is not None:
                absorb_part(*absorb, mp, c)

    bufs = ((s_a, 0), (s_b, 1))
    dbuf = (s_d, 2)
    next_diag = (nxt_blk, q_next, dbuf, True)

    @pl.when(qi == 0)
    def _():
        pair((qi, q_maps, dbuf, True), None)
        pair(None, (qi, dbuf))
        pair(next_diag, None)

    @pl.when(qi >= 1)
    def _():
        pair((0, q_maps, bufs[0], False), (qi, dbuf))

        def run(v0, count, last):
            for j in range(count):
                pair((v0 + j + 1, q_maps, bufs[(j + 1) % 2], False), (v0 + j, bufs[j % 2]))
            if last:
                pair(next_diag, (v0 + count, bufs[count % 2]))

        def body(t, carry):
            run(ATT_UNROLL * t, ATT_UNROLL, False)
            return carry

        n_pairs = qi - 1
        lax.fori_loop(0, n_pairs // ATT_UNROLL, body, 0)
        done = (n_pairs // ATT_UNROLL) * ATT_UNROLL
        for r in range(ATT_UNROLL):
            @pl.when(n_pairs - done == r)
            def _(r=r):
                run(done, r, True)

    lam = (jnp.exp(jnp.sum(lq1_ref[...] * lk1_ref[...], axis=-1, keepdims=True))
           - jnp.exp(jnp.sum(lq2_ref[...] * lk2_ref[...], axis=-1, keepdims=True)) + LAM_INIT)
    o_t = acc_sc[0] / ml_sc[1, 0] - lam * (acc_sc[1] / ml_sc[1, 1])
    o = _rms_rows(o_t.T, sub_ref[...]) * (1.0 - LAM_INIT)
    g = gate_ref[0].astype(F32)
    o_ref[0] = (o * (g * jax.nn.sigmoid(g))).astype(BF16)


def _attention(q, k, vt, gate, p):
    bsz, seq, _ = q.shape
    tq = ATT_TQ
    qblk = pl.BlockSpec((1, tq, DIFF_DV), lambda b, h, i: (b, i, h))
    kblk = pl.BlockSpec((1, seq, DIFF_DV), lambda b, h, i: (b, 0, h))
    vtblk = pl.BlockSpec((1, DIFF_DV, seq), lambda b, h, i: (b, h, 0))
    vec = pl.BlockSpec((1, DIFF_DK), lambda b, h, i: (0, 0))
    return pl.pallas_call(
        _attn_kernel,
        grid=(bsz, DIFF_HEADS, seq // tq),
        in_specs=[kblk, kblk, vtblk, qblk, vec, vec, vec, vec,
                  pl.BlockSpec((1, DIFF_DV), lambda b, h, i: (0, 0))],
        out_specs=qblk,
        out_shape=jax.ShapeDtypeStruct((bsz, seq, DIFF_HEADS * DIFF_DV), BF16),
        scratch_shapes=[
            pltpu.VMEM((2, DIFF_DV, tq), F32),
            pltpu.VMEM((2, ATT_TK, tq), F32),
            pltpu.VMEM((2, ATT_TK, tq), F32),
            pltpu.VMEM((2, ATT_TK, tq), F32),
            pltpu.VMEM((3, 2, 1, tq), F32),
            pltpu.VMEM((2, 2, 1, tq), F32),
        ],
        compiler_params=pltpu.CompilerParams(
            dimension_semantics=("arbitrary", "arbitrary", "arbitrary"),
            vmem_limit_bytes=VMEM_LIMIT_BYTES),
        name="diff_attn",
    )(q, k, vt, gate, p["lq1"], p["lk1"], p["lq2"], p["lk2"], p["subln_w"])


def _even_params(norm_w, w_in, conv_w, conv_b, dt_bias, a_log, d_skip, ssd_norm_w,
                 dw_w, dw_b, ln_w, ln_b, w_out):
    d_ssd = SSD_HEADS * SSD_HEADDIM
    xbc_dim = d_ssd + 2 * SSD_GROUPS * D_STATE
    o_za, o_xbc, o_dt = 0, d_ssd, d_ssd + xbc_dim
    o_gv = o_dt + SSD_HEADS
    o_gg, o_zb = o_gv + D_MODEL, o_gv + 2 * D_MODEL
    w_main = jnp.concatenate([
        w_in[:, o_xbc:o_xbc + xbc_dim], w_in[:, o_za:o_za + d_ssd],
        w_in[:, o_gv:o_gv + D_MODEL], w_in[:, o_gg:o_gg + D_MODEL], w_in[:, o_zb:o_zb + D_MODEL],
    ], axis=1).astype(BF16)
    pad = LANES - SSD_HEADS
    w_dt = jnp.pad(w_in[:, o_dt:o_dt + SSD_HEADS], ((0, 0), (0, pad))).astype(BF16)

    heads = np.arange(LANES)[:, None]
    e64 = (heads == (np.arange(d_ssd)[None, :] // SSD_HEADDIM)).astype(np.float32)
    e128 = (heads == (np.arange(SSD_HEADS * LANES)[None, :] // LANES)).astype(np.float32)
    r = np.arange(MIX_ROWS)
    same = (r[:, None] // CHUNK) == (r[None, :] // CHUNK)
    tri = (same & (r[None, :] <= r[:, None])).astype(np.float32)
    return dict(
        norm_w=norm_w[None, :], w_main=w_main, w_dt=w_dt,
        conv_w=conv_w, conv_b=conv_b[None, :],
        dt_bias=jnp.pad(dt_bias, (0, pad))[None, :], a_log=jnp.pad(a_log, (0, pad))[None, :],
        d_skip=jnp.repeat(d_skip, SSD_HEADDIM)[None, :], ssd_norm_w=ssd_norm_w[None, :],
        dw_w=jnp.broadcast_to(dw_w.reshape(CONF_KERNEL, D_MODEL // LANES, 1, LANES),
                              (CONF_KERNEL, D_MODEL // LANES, SUBLANES, LANES)),
        dw_b=jnp.broadcast_to(dw_b.reshape(D_MODEL // LANES, 1, LANES), (D_MODEL // LANES, SUBLANES, LANES)),
        ln_w=ln_w[None, :], ln_b=ln_b[None, :],
        e64=jnp.asarray(np.concatenate([e64, e64]), BF16), e128=jnp.asarray(np.concatenate([e128, e128]), BF16),
        tri=jnp.asarray(tri, BF16), blk=jnp.asarray(same.astype(np.float32), BF16),
        w_out=w_out.astype(BF16),
    )


def _odd_params(norm_w, w_in, q_norm_w, k_norm_w, lq1, lk1, lq2, lk2, subln_w, w_out):
    maps = D_MODEL // DIFF_DK
    lane = np.arange(LANES) % DIFF_DK
    half = ROT_DIM // 2
    inv = (ROPE_THETA ** (-2.0 * jnp.arange(half, dtype=F32) / ROT_DIM))
    rot = lane < ROT_DIM
    sgn = np.where(lane < half, -1.0, 1.0).astype(np.float32)[None, :]
    sel = ((np.arange(LANES)[:, None] == (lane % half)[None, :]) & rot[None, :]).astype(np.float32)
    one = (~rot).astype(np.float32)[None, :]
    gblk = 2 * LANES
    gi = np.arange(gblk) // DIFF_DK
    gmean = (gi[:, None] == gi[None, :]).astype(np.float32) / DIFF_DK
    return dict(
        norm_w=norm_w[None, :],
        w_qkg=jnp.concatenate([w_in[:, :2 * D_MODEL], w_in[:, 3 * D_MODEL:]], axis=1).astype(BF16),
        w_vt=w_in[:, 2 * D_MODEL:3 * D_MODEL].T.astype(BF16),
        q_norm_w=jnp.tile(q_norm_w, maps)[None, :], k_norm_w=jnp.tile(k_norm_w, maps)[None, :],
        invf=inv.astype(F32)[:, None], sgn=jnp.asarray(sgn),
        sel=jnp.asarray(np.concatenate([sel, sel]), BF16), one=jnp.asarray(one), gmean=jnp.asarray(gmean, BF16),
        lq1=lq1[None, :], lk1=lk1[None, :], lq2=lq2[None, :], lk2=lk2[None, :],
        subln_w=subln_w[None, :], w_out=w_out.astype(BF16),
    )


def kernel(x, positions, a_norm_w, a_w_in, a_conv_w, a_conv_b, a_dt_bias, a_a_log, a_d_skip,
           a_ssd_norm_w, a_dw_w, a_dw_b, a_ln_w, a_ln_b, a_w_out, c_norm_w, c_w_in, c_q_norm_w,
           c_k_norm_w, c_lq1, c_lk1, c_lq2, c_lk2, c_subln_w, c_w_out):
    bsz, seq, d = x.shape
    assert d == D_MODEL and seq % max(MIX_ROWS, ATT_TQ, PROJ_ROWS) == 0
    assert a_norm_w.shape[0] == 1 and c_norm_w.shape[0] == 1
    m = bsz * seq
    x2 = x.reshape(m, d)

    pe = _even_params(a_norm_w[0], a_w_in[0], a_conv_w[0], a_conv_b[0], a_dt_bias[0], a_a_log[0],
                      a_d_skip[0], a_ssd_norm_w[0], a_dw_w[0], a_dw_b[0], a_ln_w[0], a_ln_b[0], a_w_out[0])
    proj, dt_raw = _even_inproj(x2, pe["norm_w"], pe["w_main"], pe["w_dt"])
    y = _even_mix(proj, dt_raw, pe, bsz, seq)
    x2 = _outproj(y, pe["w_out"], x2, "even_outproj")

    po = _odd_params(c_norm_w[0], c_w_in[0], c_q_norm_w[0], c_k_norm_w[0], c_lq1[0], c_lk1[0],
                     c_lq2[0], c_lk2[0], c_subln_w[0], c_w_out[0])
    q, k, vt, gate = _odd_inproj(x2, positions.reshape(m, 1), po, bsz, seq)
    shp = (bsz, seq, d)
    o = _attention(q.reshape(shp), k.reshape(shp), vt, gate.reshape(shp), po)
    x2 = _outproj(o.reshape(m, d), po["w_out"], x2, "odd_outproj")
    return x2.reshape(bsz, seq, d)
```

```python
import functools
import math

import jax
import jax.numpy as jnp
import numpy as np
from jax import lax
from jax.experimental import pallas as pl
from jax.experimental.pallas import tpu as pltpu

F32 = jnp.float32
BF16 = jnp.bfloat16

D_MODEL = 1024
CHUNK = 64
SSD_HEADS = 16
SSD_HEADDIM = 64
SSD_GROUPS = 4
HEADS_PER_GROUP = SSD_HEADS // SSD_GROUPS
D_STATE = 128
SSD_CONV = 4
CONF_KERNEL = 31
DIFF_HEADS = 8
DIFF_DK = 64
DIFF_DV = 128
ROT_DIM = 16
ROPE_THETA = 500000.0
EPS = 1e-6
LN_EPS = 1e-5
ODD_LAYER_IDX = 1
LAM_INIT = 0.8 - 0.6 * math.exp(-0.3 * ODD_LAYER_IDX)

LANES = 128
SUBLANES = 8
VMEM_LIMIT_BYTES = 56 * 1024 * 1024

PROJ_ROWS = 512
MIX_ROWS = 256
PAIR = 2 * CHUNK
XCARRY = 8
UCARRY = 32
CONV_ROWS = 128
ATT_TQ = 512
ATT_TK = 512
ATT_UNROLL = 6
NEG = -1e30


def _dot(a, b):
    return jnp.dot(a, b, preferred_element_type=F32)


def _dot_nt(a, b):
    return lax.dot_general(a, b, (((1,), (1,)), ((), ())), preferred_element_type=F32)


def _split2(x):
    hi = x.astype(BF16)
    lo = (x - hi.astype(F32)).astype(BF16)
    return hi, lo


def _expand(x, e2_ref):
    hi, lo = _split2(x)
    return _dot(jnp.concatenate([hi, lo], axis=1), e2_ref[...])


def _left_apply(m_ref, x):
    hi, lo = _split2(x)
    n = x.shape[1]
    r = _dot(m_ref[...], jnp.concatenate([hi, lo], axis=1))
    return r[:, :n] + r[:, n:]


def _rms_rows(x, w):
    ms = jnp.mean(x * x, axis=-1, keepdims=True)
    return x * lax.rsqrt(ms + EPS) * w


def _even_inproj_kernel(x_ref, nw_ref, w_ref, wdt_ref, proj_ref, dt_ref):
    hn = _rms_rows(x_ref[...], nw_ref[...]).astype(BF16)
    n_out = proj_ref.shape[1]
    for c in range(0, n_out, D_MODEL):
        proj_ref[:, c:c + D_MODEL] = _dot(hn, w_ref[:, c:c + D_MODEL]).astype(BF16)
    dt_ref[...] = _dot(hn, wdt_ref[...])


def _even_inproj(x2, nw, w, wdt):
    m = x2.shape[0]
    n_out = w.shape[1]
    tm = PROJ_ROWS
    return pl.pallas_call(
        _even_inproj_kernel,
        grid=(m // tm,),
        in_specs=[
            pl.BlockSpec((tm, D_MODEL), lambda i: (i, 0)),
            pl.BlockSpec((1, D_MODEL), lambda i: (0, 0)),
            pl.BlockSpec((D_MODEL, n_out), lambda i: (0, 0), pipeline_mode=pl.Buffered(1)),
            pl.BlockSpec((D_MODEL, LANES), lambda i: (0, 0), pipeline_mode=pl.Buffered(1)),
        ],
        out_specs=[
            pl.BlockSpec((tm, n_out), lambda i: (i, 0)),
            pl.BlockSpec((tm, LANES), lambda i: (i, 0)),
        ],
        out_shape=[
            jax.ShapeDtypeStruct((m, n_out), BF16),
            jax.ShapeDtypeStruct((m, LANES), F32),
        ],
        compiler_params=pltpu.CompilerParams(
            dimension_semantics=("arbitrary",), vmem_limit_bytes=VMEM_LIMIT_BYTES),
        name="even_inproj",
    )(x2, nw, w, wdt)


def _even_mix_kernel(xbc_ref, za_ref, gv_ref, gg_ref, zb_ref, dt_ref,
                     cw_ref, cb_ref, dtb_ref, alog_ref, dskip_ref, nrm_ref,
                     dww_ref, dwb_ref, lnw_ref, lnb_ref,
                     e64_ref, e128_ref, tri_ref, blk_ref,
                     out_ref,
                     xbuf, ubuf, hs, ybuf, uph, cbuf):
    t = pl.program_id(1)
    rows = za_ref.shape[0]
    d_ssd = SSD_HEADS * SSD_HEADDIM
    gw = HEADS_PER_GROUP * SSD_HEADDIM

    @pl.when(t == 0)
    def _():
        xbuf[0:XCARRY, :] = jnp.zeros((XCARRY, xbuf.shape[1]), F32)
        ubuf[0:UCARRY, :] = jnp.zeros((UCARRY, ubuf.shape[1]), F32)
        hs[...] = jnp.zeros(hs.shape, F32)

    xbuf[XCARRY:XCARRY + rows, :] = xbc_ref[...].astype(F32)
    acc = cb_ref[...]
    for k in range(SSD_CONV):
        off = XCARRY - (SSD_CONV - 1) + k
        acc = acc + cw_ref[k:k + 1, :] * xbuf[off:off + rows, :]
    xbuf[0:XCARRY, :] = xbuf[rows:rows + XCARRY, :]
    xc = acc * jax.nn.sigmoid(acc)
    xs = xc[:, :d_ssd]
    bm = xc[:, d_ssd:d_ssd + SSD_GROUPS * D_STATE]
    cm = xc[:, d_ssd + SSD_GROUPS * D_STATE:]

    dtr = dt_ref[...] + dtb_ref[...]
    dt = jnp.maximum(dtr, 0.0) + jnp.log1p(jnp.exp(-jnp.abs(dtr)))
    da = dt * (-jnp.exp(alog_ref[...]))
    cs = _left_apply(tri_ref, da)
    tot = _left_apply(blk_ref, da)
    dt_x = _expand(dt, e64_ref)
    ecs_x = _expand(jnp.exp(cs), e64_ref)
    edst_x = _expand(jnp.exp(tot - cs), e64_ref)
    ecd_x = _expand(jnp.exp(tot), e64_ref)
    cs_col = _expand(cs, e128_ref)
    cs_t = cs.T

    xd = xs * dt_x
    xdd = xd * edst_x

    li = lax.broadcasted_iota(jnp.int32, (PAIR, PAIR), 0)
    si = lax.broadcasted_iota(jnp.int32, (PAIR, PAIR), 1)
    intra = (si <= li) & ((si // CHUNK) == (li // CHUNK))
    lane_g = lax.broadcasted_iota(jnp.int32, (PAIR, gw), 1)
    row_g = lax.broadcasted_iota(jnp.int32, (PAIR, gw), 0)

    for pr in range(rows // PAIR):
        r0 = pr * PAIR
        for g in range(SSD_GROUPS):
            c_g = cm[r0:r0 + PAIR, g * D_STATE:(g + 1) * D_STATE].astype(BF16)
            b_g = bm[r0:r0 + PAIR, g * D_STATE:(g + 1) * D_STATE]
            cb = _dot_nt(c_g, b_g.astype(BF16))
            b_gt = b_g.T.astype(BF16)
            xd_g = xd[r0:r0 + PAIR, g * gw:(g + 1) * gw]
            xdd_g = xdd[r0:r0 + PAIR, g * gw:(g + 1) * gw]
            m_parts = []
            w_parts = []
            for e in range(HEADS_PER_GROUP):
                h = g * HEADS_PER_GROUP + e
                seg = cs_col[r0:r0 + PAIR, h * LANES:(h + 1) * LANES] - cs_t[h:h + 1, r0:r0 + PAIR]
                decay = jnp.exp(jnp.where(intra, seg, NEG))
                m_parts.append((cb * decay).astype(BF16))
                in_head = (lane_g // SSD_HEADDIM) == e
                w_parts.append(jnp.where(in_head, xd_g, 0.0).astype(BF16))
            y_diag = _dot(jnp.concatenate(m_parts, axis=1), jnp.concatenate(w_parts, axis=0))
            for j in range(PAIR // CHUNK):
                rj = r0 + j * CHUNK
                h_prev = hs[g]
                y_off = _dot(c_g[j * CHUNK:(j + 1) * CHUNK], h_prev.astype(BF16))
                y_off = y_off * ecs_x[rj:rj + CHUNK, g * gw:(g + 1) * gw]
                ybuf[rj:rj + CHUNK, g * gw:(g + 1) * gw] = y_diag[j * CHUNK:(j + 1) * CHUNK] + y_off
                in_chunk = (row_g // CHUNK) == j
                st = _dot(b_gt, jnp.where(in_chunk, xdd_g, 0.0).astype(BF16))
                hs[g] = h_prev * ecd_x[rj:rj + 1, g * gw:(g + 1) * gw] + st

    y = ybuf[...] + xs * dskip_ref[...]
    za = za_ref[...].astype(F32)
    y = y * (za * jax.nn.sigmoid(za))
    for g in range(SSD_GROUPS):
        yg = y[:, g * gw:(g + 1) * gw]
        out_ref[:, g * gw:(g + 1) * gw] = _rms_rows(yg, nrm_ref[:, g * gw:(g + 1) * gw]).astype(BF16)

    gg = gg_ref[...].astype(F32)
    ubuf[UCARRY:UCARRY + rows, :] = gv_ref[...].astype(F32) * jax.nn.sigmoid(gg)
    base = UCARRY - (CONF_KERNEL - 1)
    ntile = D_MODEL // LANES
    for sh in range(SUBLANES):
        span = UCARRY + rows - (SUBLANES if sh else 0)
        for c in range(ntile):
            uph[sh, c, 0:span, :] = ubuf[sh:sh + span, c * LANES:(c + 1) * LANES]
    ubuf[0:UCARRY, :] = ubuf[rows:rows + UCARRY, :]
    nblk = rows // CONV_ROWS

    def conv_block(i, carry):
        c = i // nblk
        r0 = pl.multiple_of((i % nblk) * CONV_ROWS, CONV_ROWS)
        nsub = CONV_ROWS // SUBLANES
        a = jnp.broadcast_to(dwb_ref[c], (nsub, SUBLANES, LANES))
        for k in range(CONF_KERNEL):
            sh = (base + k) % SUBLANES
            start = pl.multiple_of(r0 + (base + k - sh), SUBLANES)
            blk = uph[sh, c, pl.ds(start, CONV_ROWS), :].reshape(nsub, SUBLANES, LANES)
            a = a + dww_ref[k, c] * blk
        cbuf[c, pl.ds(r0, CONV_ROWS), :] = a.reshape(CONV_ROWS, LANES)
        return carry

    lax.fori_loop(0, ntile * nblk, conv_block, 0)
    acc_u = jnp.concatenate([cbuf[c] for c in range(ntile)], axis=1)
    mu = jnp.mean(acc_u, axis=-1, keepdims=True)
    cen = acc_u - mu
    var = jnp.mean(cen * cen, axis=-1, keepdims=True)
    un = cen * lax.rsqrt(var + LN_EPS) * lnw_ref[...] + lnb_ref[...]
    un = un * jax.nn.sigmoid(un)
    zb = zb_ref[...].astype(F32)
    out_ref[:, d_ssd:] = (un * (zb * jax.nn.sigmoid(zb))).astype(BF16)


def _even_mix(proj, dt_raw, p, bsz, seq):
    rows = MIX_ROWS
    nt = seq // rows
    d_ssd = SSD_HEADS * SSD_HEADDIM
    xbc_dim = d_ssd + 2 * SSD_GROUPS * D_STATE
    m = bsz * seq

    def rowblk(col):
        return lambda b, t: (b * nt + t, col)

    def const(shape):
        return pl.BlockSpec(shape, lambda b, t: (0, 0))

    in_specs = [
        pl.BlockSpec((rows, xbc_dim), rowblk(0)),
        pl.BlockSpec((rows, D_MODEL), rowblk(2)),
        pl.BlockSpec((rows, D_MODEL), rowblk(3)),
        pl.BlockSpec((rows, D_MODEL), rowblk(4)),
        pl.BlockSpec((rows, D_MODEL), rowblk(5)),
        pl.BlockSpec((rows, LANES), rowblk(0)),
        const((SSD_CONV, xbc_dim)), const((1, xbc_dim)),
        const((1, LANES)), const((1, LANES)),
        const((1, d_ssd)), const((1, d_ssd)),
        pl.BlockSpec((CONF_KERNEL, D_MODEL // LANES, SUBLANES, LANES), lambda b, t: (0, 0, 0, 0)),
        pl.BlockSpec((D_MODEL // LANES, SUBLANES, LANES), lambda b, t: (0, 0, 0)),
        const((1, D_MODEL)), const((1, D_MODEL)),
        const((2 * LANES, d_ssd)), const((2 * LANES, SSD_HEADS * LANES)),
        const((rows, rows)), const((rows, rows)),
    ]
    return pl.pallas_call(
        _even_mix_kernel,
        grid=(bsz, nt),
        in_specs=in_specs,
        out_specs=pl.BlockSpec((rows, 2 * D_MODEL), rowblk(0)),
        out_shape=jax.ShapeDtypeStruct((m, 2 * D_MODEL), BF16),
        scratch_shapes=[
            pltpu.VMEM((XCARRY + rows, xbc_dim), F32),
            pltpu.VMEM((UCARRY + rows, D_MODEL), F32),
            pltpu.VMEM((SSD_GROUPS, D_STATE, HEADS_PER_GROUP * SSD_HEADDIM), F32),
            pltpu.VMEM((rows, d_ssd), F32),
            pltpu.VMEM((SUBLANES, D_MODEL // LANES, UCARRY + rows, LANES), F32),
            pltpu.VMEM((D_MODEL // LANES, rows, LANES), F32),
        ],
        compiler_params=pltpu.CompilerParams(
            dimension_semantics=("arbitrary", "arbitrary"), vmem_limit_bytes=VMEM_LIMIT_BYTES),
        name="even_mix",
    )(proj, proj, proj, proj, proj, dt_raw,
      p["conv_w"], p["conv_b"], p["dt_bias"], p["a_log"], p["d_skip"], p["ssd_norm_w"],
      p["dw_w"], p["dw_b"], p["ln_w"], p["ln_b"],
      p["e64"], p["e128"], p["tri"], p["blk"])


def _outproj_kernel(y_ref, w_ref, x_ref, o_ref):
    o_ref[...] = x_ref[...] + _dot(y_ref[...], w_ref[...])


def _outproj(y, w, x2, name):
    m, kdim = y.shape
    tm = PROJ_ROWS
    return pl.pallas_call(
        _outproj_kernel,
        grid=(m // tm,),
        in_specs=[
            pl.BlockSpec((tm, kdim), lambda i: (i, 0)),
            pl.BlockSpec((kdim, D_MODEL), lambda i: (0, 0), pipeline_mode=pl.Buffered(1)),
            pl.BlockSpec((tm, D_MODEL), lambda i: (i, 0)),
        ],
        out_specs=pl.BlockSpec((tm, D_MODEL), lambda i: (i, 0)),
        out_shape=jax.ShapeDtypeStruct((m, D_MODEL), F32),
        compiler_params=pltpu.CompilerParams(
            dimension_semantics=("arbitrary",), vmem_limit_bytes=VMEM_LIMIT_BYTES),
        name=name,
    )(y, w, x2)


def _odd_inproj_kernel(x_ref, pos_ref, nw_ref, w_ref, wvt_ref, qnw_ref, knw_ref, invf_ref, sgn_ref,
                       sel_ref, one_ref, gmean_ref, q_ref, k_ref, vt_ref, g_ref):
    hn = _rms_rows(x_ref[...], nw_ref[...]).astype(BF16)
    d = D_MODEL
    half = ROT_DIM // 2
    ang_t = invf_ref[...] * pos_ref[0].astype(F32)
    pad = jnp.zeros((LANES - half, ang_t.shape[1]), F32)
    cos_c = jnp.concatenate([jnp.cos(ang_t), pad], axis=0).T
    sin_c = jnp.concatenate([jnp.sin(ang_t), pad], axis=0).T
    reps = d // LANES
    cos = jnp.concatenate([_expand(cos_c, sel_ref) + one_ref[...]] * reps, axis=1)
    sin = jnp.concatenate([_expand(sin_c, sel_ref) * sgn_ref[...]] * reps, axis=1)
    lane = lax.broadcasted_iota(jnp.int32, (1, d), 1) % DIFF_DK
    first_half = lane < (ROT_DIM // 2)
    gblk = gmean_ref.shape[0]

    def norm_rot(z, w_row, scale):
        sq = (z * z).astype(BF16)
        parts = []
        for c in range(0, d, gblk):
            parts.append(_dot(sq[:, c:c + gblk], gmean_ref[...]))
        ms = jnp.concatenate(parts, axis=1)
        zn = z * lax.rsqrt(ms + EPS) * w_row
        swapped = jnp.where(first_half, pltpu.roll(zn, d - ROT_DIM // 2, 1), pltpu.roll(zn, ROT_DIM // 2, 1))
        out = zn * cos + swapped * sin
        if scale != 1.0:
            out = out * scale
        return out.astype(BF16)

    q = _dot(hn, w_ref[:, 0:d])
    k = _dot(hn, w_ref[:, d:2 * d])
    q_ref[...] = norm_rot(q, qnw_ref[...], math.log2(math.e) * DIFF_DK ** -0.5)
    vt_ref[0] = _dot_nt(wvt_ref[...], hn).astype(BF16)
    k_ref[...] = norm_rot(k, knw_ref[...], 1.0)
    g_ref[...] = _dot(hn, w_ref[:, 2 * d:3 * d]).astype(BF16)


def _odd_inproj(x2, pos2, p, bsz, seq):
    m = x2.shape[0]
    tm = PROJ_ROWS
    nt = seq // tm
    row = lambda b, i: (b * nt + i, 0)
    const = lambda b, i: (0, 0)
    gblk = p["gmean"].shape[0]
    out_sds = jax.ShapeDtypeStruct((m, D_MODEL), BF16)
    return pl.pallas_call(
        _odd_inproj_kernel,
        grid=(bsz, nt),
        in_specs=[
            pl.BlockSpec((tm, D_MODEL), row),
            pl.BlockSpec((1, 1, tm), lambda b, i: (b * nt + i, 0, 0)),
            pl.BlockSpec((1, D_MODEL), const),
            pl.BlockSpec((D_MODEL, 3 * D_MODEL), const, pipeline_mode=pl.Buffered(1)),
            pl.BlockSpec((D_MODEL, D_MODEL), const, pipeline_mode=pl.Buffered(1)),
            pl.BlockSpec((1, D_MODEL), const),
            pl.BlockSpec((1, D_MODEL), const),
            pl.BlockSpec((ROT_DIM // 2, 1), const),
            pl.BlockSpec((1, LANES), const),
            pl.BlockSpec((2 * LANES, LANES), const),
            pl.BlockSpec((1, LANES), const),
            pl.BlockSpec((gblk, gblk), const),
        ],
        out_specs=[pl.BlockSpec((tm, D_MODEL), row), pl.BlockSpec((tm, D_MODEL), row),
                   pl.BlockSpec((1, D_MODEL, tm), lambda b, i: (b, 0, i)),
                   pl.BlockSpec((tm, D_MODEL), row)],
        out_shape=[out_sds, out_sds, jax.ShapeDtypeStruct((bsz, D_MODEL, seq), BF16), out_sds],
        compiler_params=pltpu.CompilerParams(
            dimension_semantics=("arbitrary", "arbitrary"), vmem_limit_bytes=VMEM_LIMIT_BYTES),
        name="odd_inproj",
    )(x2, pos2.reshape(m // tm, 1, tm), p["norm_w"], p["w_qkg"], p["w_vt"], p["q_norm_w"], p["k_norm_w"],
      p["invf"], p["sgn"], p["sel"], p["one"], p["gmean"])


def _attn_kernel(q_ref, k_ref, vt_ref, gate_ref, lq1_ref, lk1_ref, lq2_ref, lk2_ref, sub_ref,
                 o_ref, acc_sc, s_a, s_b, tmax_sc, ml_sc):
    qi = pl.program_id(2)
    tq = q_ref.shape[1]
    tk = ATT_TK
    assert tq == tk
    q = q_ref[0]
    lane = lax.broadcasted_iota(jnp.int32, q.shape, 1)
    zero = jnp.zeros_like(q)
    q_maps = (jnp.where(lane < DIFF_DK, q, zero), jnp.where(lane >= DIFF_DK, q, zero))

    acc_sc[...] = jnp.zeros(acc_sc.shape, F32)
    ml_sc[0] = jnp.full(ml_sc.shape[1:], NEG, F32)
    ml_sc[1] = jnp.zeros(ml_sc.shape[1:], F32)

    def tile_start(u):
        return pl.multiple_of(jnp.where(u == 0, qi, u - 1) * tk, tk)

    hq = tq // 2

    def scores_part(u, s_buf, slot, masked, mp, c):
        kj = k_ref[0, pl.ds(tile_start(u), tk), :]
        s = _dot_nt(kj, q_maps[mp][c:c + hq])
        if masked:
            key = lax.broadcasted_iota(jnp.int32, (tk, hq), 0)
            qry = lax.broadcasted_iota(jnp.int32, (tk, hq), 1) + c
            s = jnp.where((key // CHUNK) <= (qry // CHUNK), s, NEG)
        s_buf[mp, :, c:c + hq] = s
        tmax_sc[slot, mp, :, c:c + hq] = jnp.max(s, axis=0, keepdims=True)

    def scores(u, s_buf, slot, masked):
        for mp in range(2):
            for c in (0, hq):
                scores_part(u, s_buf, slot, masked, mp, c)

    def absorb_part(u, s_buf, slot, mp, c):
        vtj = vt_ref[0, :, pl.ds(tile_start(u), tk)]
        m_prev = ml_sc[0, mp, :, c:c + hq]
        m_new = jnp.maximum(m_prev, tmax_sc[slot, mp, :, c:c + hq])
        p = jnp.exp2(s_buf[mp, :, c:c + hq] - m_new)
        alpha = jnp.exp2(m_prev - m_new)
        ml_sc[1, mp, :, c:c + hq] = alpha * ml_sc[1, mp, :, c:c + hq] + jnp.sum(p, axis=0, keepdims=True)
        acc_sc[mp, :, c:c + hq] = alpha * acc_sc[mp, :, c:c + hq] + _dot(vtj, p.astype(BF16))
        ml_sc[0, mp, :, c:c + hq] = m_new

    def absorb(u, s_buf, slot):
        for mp in range(2):
            for c in (0, hq):
                absorb_part(u, s_buf, slot, mp, c)

    scores(0, s_a, 0, True)

    bufs = ((s_a, 0), (s_b, 1))
    assert ATT_UNROLL % 2 == 0

    def steps(u0, count, tail):
        for j in range(count):
            nxt, cur = bufs[(j + 1) % 2], bufs[j % 2]
            for mp in range(2):
                for c in (0, hq):
                    scores_part(u0 + j + 1, nxt[0], nxt[1], False, mp, c)
                    absorb_part(u0 + j, cur[0], cur[1], mp, c)
        if tail:
            last = bufs[count % 2]
            absorb(u0 + count, last[0], last[1])

    def body(t, carry):
        steps(ATT_UNROLL * t, ATT_UNROLL, False)
        return carry

    lax.fori_loop(0, qi // ATT_UNROLL, body, 0)
    done = (qi // ATT_UNROLL) * ATT_UNROLL
    for r in range(ATT_UNROLL):
        @pl.when(qi - done == r)
        def _(r=r):
            steps(done, r, True)

    lam = (jnp.exp(jnp.sum(lq1_ref[...] * lk1_ref[...], axis=-1, keepdims=True))
           - jnp.exp(jnp.sum(lq2_ref[...] * lk2_ref[...], axis=-1, keepdims=True)) + LAM_INIT)
    o_t = acc_sc[0] / ml_sc[1, 0] - lam * (acc_sc[1] / ml_sc[1, 1])
    o = _rms_rows(o_t.T, sub_ref[...]) * (1.0 - LAM_INIT)
    g = gate_ref[0].astype(F32)
    o_ref[0] = (o * (g * jax.nn.sigmoid(g))).astype(BF16)


def _attention(q, k, vt, gate, p):
    bsz, seq, _ = q.shape
    tq = ATT_TQ
    qblk = pl.BlockSpec((1, tq, DIFF_DV), lambda b, h, i: (b, i, h))
    kblk = pl.BlockSpec((1, seq, DIFF_DV), lambda b, h, i: (b, 0, h))
    vtblk = pl.BlockSpec((1, DIFF_DV, seq), lambda b, h, i: (b, h, 0))
    vec = pl.BlockSpec((1, DIFF_DK), lambda b, h, i: (0, 0))
    return pl.pallas_call(
        _attn_kernel,
        grid=(bsz, DIFF_HEADS, seq // tq),
        in_specs=[qblk, kblk, vtblk, qblk, vec, vec, vec, vec,
                  pl.BlockSpec((1, DIFF_DV), lambda b, h, i: (0, 0))],
        out_specs=qblk,
        out_shape=jax.ShapeDtypeStruct((bsz, seq, DIFF_HEADS * DIFF_DV), BF16),
        scratch_shapes=[
            pltpu.VMEM((2, DIFF_DV, tq), F32),
            pltpu.VMEM((2, ATT_TK, tq), F32),
            pltpu.VMEM((2, ATT_TK, tq), F32),
            pltpu.VMEM((2, 2, 1, tq), F32),
            pltpu.VMEM((2, 2, 1, tq), F32),
        ],
        compiler_params=pltpu.CompilerParams(
            dimension_semantics=("arbitrary", "arbitrary", "arbitrary"),
            vmem_limit_bytes=VMEM_LIMIT_BYTES),
        name="diff_attn",
    )(q, k, vt, gate, p["lq1"], p["lk1"], p["lq2"], p["lk2"], p["subln_w"])


def _even_params(norm_w, w_in, conv_w, conv_b, dt_bias, a_log, d_skip, ssd_norm_w,
                 dw_w, dw_b, ln_w, ln_b, w_out):
    d_ssd = SSD_HEADS * SSD_HEADDIM
    xbc_dim = d_ssd + 2 * SSD_GROUPS * D_STATE
    o_za, o_xbc, o_dt = 0, d_ssd, d_ssd + xbc_dim
    o_gv = o_dt + SSD_HEADS
    o_gg, o_zb = o_gv + D_MODEL, o_gv + 2 * D_MODEL
    w_main = jnp.concatenate([
        w_in[:, o_xbc:o_xbc + xbc_dim], w_in[:, o_za:o_za + d_ssd],
        w_in[:, o_gv:o_gv + D_MODEL], w_in[:, o_gg:o_gg + D_MODEL], w_in[:, o_zb:o_zb + D_MODEL],
    ], axis=1).astype(BF16)
    pad = LANES - SSD_HEADS
    w_dt = jnp.pad(w_in[:, o_dt:o_dt + SSD_HEADS], ((0, 0), (0, pad))).astype(BF16)

    heads = np.arange(LANES)[:, None]
    e64 = (heads == (np.arange(d_ssd)[None, :] // SSD_HEADDIM)).astype(np.float32)
    e128 = (heads == (np.arange(SSD_HEADS * LANES)[None, :] // LANES)).astype(np.float32)
    r = np.arange(MIX_ROWS)
    same = (r[:, None] // CHUNK) == (r[None, :] // CHUNK)
    tri = (same & (r[None, :] <= r[:, None])).astype(np.float32)
    return dict(
        norm_w=norm_w[None, :], w_main=w_main, w_dt=w_dt,
        conv_w=conv_w, conv_b=conv_b[None, :],
        dt_bias=jnp.pad(dt_bias, (0, pad))[None, :], a_log=jnp.pad(a_log, (0, pad))[None, :],
        d_skip=jnp.repeat(d_skip, SSD_HEADDIM)[None, :], ssd_norm_w=ssd_norm_w[None, :],
        dw_w=jnp.broadcast_to(dw_w.reshape(CONF_KERNEL, D_MODEL // LANES, 1, LANES),
                              (CONF_KERNEL, D_MODEL // LANES, SUBLANES, LANES)),
        dw_b=jnp.broadcast_to(dw_b.reshape(D_MODEL // LANES, 1, LANES), (D_MODEL // LANES, SUBLANES, LANES)),
        ln_w=ln_w[None, :], ln_b=ln_b[None, :],
        e64=jnp.asarray(np.concatenate([e64, e64]), BF16), e128=jnp.asarray(np.concatenate([e128, e128]), BF16),
        tri=jnp.asarray(tri, BF16), blk=jnp.asarray(same.astype(np.float32), BF16),
        w_out=w_out.astype(BF16),
    )


def _odd_params(norm_w, w_in, q_norm_w, k_norm_w, lq1, lk1, lq2, lk2, subln_w, w_out):
    maps = D_MODEL // DIFF_DK
    lane = np.arange(LANES) % DIFF_DK
    half = ROT_DIM // 2
    inv = (ROPE_THETA ** (-2.0 * jnp.arange(half, dtype=F32) / ROT_DIM))
    rot = lane < ROT_DIM
    sgn = np.where(lane < half, -1.0, 1.0).astype(np.float32)[None, :]
    sel = ((np.arange(LANES)[:, None] == (lane % half)[None, :]) & rot[None, :]).astype(np.float32)
    one = (~rot).astype(np.float32)[None, :]
    gblk = 2 * LANES
    gi = np.arange(gblk) // DIFF_DK
    gmean = (gi[:, None] == gi[None, :]).astype(np.float32) / DIFF_DK
    return dict(
        norm_w=norm_w[None, :],
        w_qkg=jnp.concatenate([w_in[:, :2 * D_MODEL], w_in[:, 3 * D_MODEL:]], axis=1).astype(BF16),
        w_vt=w_in[:, 2 * D_MODEL:3 * D_MODEL].T.astype(BF16),
        q_norm_w=jnp.tile(q_norm_w, maps)[None, :], k_norm_w=jnp.tile(k_norm_w, maps)[None, :],
        invf=inv.astype(F32)[:, None], sgn=jnp.asarray(sgn),
        sel=jnp.asarray(np.concatenate([sel, sel]), BF16), one=jnp.asarray(one), gmean=jnp.asarray(gmean, BF16),
        lq1=lq1[None, :], lk1=lk1[None, :], lq2=lq2[None, :], lk2=lk2[None, :],
        subln_w=subln_w[None, :], w_out=w_out.astype(BF16),
    )


def kernel(x, positions, a_norm_w, a_w_in, a_conv_w, a_conv_b, a_dt_bias, a_a_log, a_d_skip,
           a_ssd_norm_w, a_dw_w, a_dw_b, a_ln_w, a_ln_b, a_w_out, c_norm_w, c_w_in, c_q_norm_w,
           c_k_norm_w, c_lq1, c_lk1, c_lq2, c_lk2, c_subln_w, c_w_out):
    bsz, seq, d = x.shape
    assert d == D_MODEL and seq % max(MIX_ROWS, ATT_TQ, PROJ_ROWS) == 0
    assert a_norm_w.shape[0] == 1 and c_norm_w.shape[0] == 1
    m = bsz * seq
    x2 = x.reshape(m, d)

    pe = _even_params(a_norm_w[0], a_w_in[0], a_conv_w[0], a_conv_b[0], a_dt_bias[0], a_a_log[0],
                      a_d_skip[0], a_ssd_norm_w[0], a_dw_w[0], a_dw_b[0], a_ln_w[0], a_ln_b[0], a_w_out[0])
    proj, dt_raw = _even_inproj(x2, pe["norm_w"], pe["w_main"], pe["w_dt"])
    y = _even_mix(proj, dt_raw, pe, bsz, seq)
    x2 = _outproj(y, pe["w_out"], x2, "even_outproj")

    po = _odd_params(c_norm_w[0], c_w_in[0], c_q_norm_w[0], c_k_norm_w[0], c_lq1[0], c_lk1[0],
                     c_lq2[0], c_lk2[0], c_subln_w[0], c_w_out[0])
    q, k, vt, gate = _odd_inproj(x2, positions.reshape(m, 1), po, bsz, seq)
    shp = (bsz, seq, d)
    o = _attention(q.reshape(shp), k.reshape(shp), vt, gate.reshape(shp), po)
    x2 = _outproj(o.reshape(m, d), po["w_out"], x2, "odd_outproj")
    return x2.reshape(bsz, seq, d)
```

```python
import functools
import math

import jax
import jax.numpy as jnp
import numpy as np
from jax import lax
from jax.experimental import pallas as pl
from jax.experimental.pallas import tpu as pltpu

F32 = jnp.float32
BF16 = jnp.bfloat16

D_MODEL = 1024
CHUNK = 64
SSD_HEADS = 16
SSD_HEADDIM = 64
SSD_GROUPS = 4
HEADS_PER_GROUP = SSD_HEADS // SSD_GROUPS
D_STATE = 128
SSD_CONV = 4
CONF_KERNEL = 31
DIFF_HEADS = 8
DIFF_DK = 64
DIFF_DV = 128
ROT_DIM = 16
ROPE_THETA = 500000.0
EPS = 1e-6
LN_EPS = 1e-5
ODD_LAYER_IDX = 1
LAM_INIT = 0.8 - 0.6 * math.exp(-0.3 * ODD_LAYER_IDX)

LANES = 128
SUBLANES = 8
VMEM_LIMIT_BYTES = 56 * 1024 * 1024

PROJ_ROWS = 512
MIX_ROWS = 256
PAIR = 2 * CHUNK
XCARRY = 8
UCARRY = 32
CONV_ROWS = 128
ATT_TQ = 512
ATT_TK = 512
ATT_UNROLL = 8
NEG = -1e30


def _dot(a, b):
    return jnp.dot(a, b, preferred_element_type=F32)


def _dot_nt(a, b):
    return lax.dot_general(a, b, (((1,), (1,)), ((), ())), preferred_element_type=F32)


def _split2(x):
    hi = x.astype(BF16)
    lo = (x - hi.astype(F32)).astype(BF16)
    return hi, lo


def _expand(x, e2_ref):
    hi, lo = _split2(x)
    return _dot(jnp.concatenate([hi, lo], axis=1), e2_ref[...])


def _left_apply(m_ref, x):
    hi, lo = _split2(x)
    n = x.shape[1]
    r = _dot(m_ref[...], jnp.concatenate([hi, lo], axis=1))
    return r[:, :n] + r[:, n:]


def _rms_rows(x, w):
    ms = jnp.mean(x * x, axis=-1, keepdims=True)
    return x * lax.rsqrt(ms + EPS) * w


def _even_inproj_kernel(x_ref, nw_ref, w_ref, wdt_ref, proj_ref, dt_ref):
    hn = _rms_rows(x_ref[...], nw_ref[...]).astype(BF16)
    n_out = proj_ref.shape[1]
    for c in range(0, n_out, D_MODEL):
        proj_ref[:, c:c + D_MODEL] = _dot(hn, w_ref[:, c:c + D_MODEL]).astype(BF16)
    dt_ref[...] = _dot(hn, wdt_ref[...])


def _even_inproj(x2, nw, w, wdt):
    m = x2.shape[0]
    n_out = w.shape[1]
    tm = PROJ_ROWS
    return pl.pallas_call(
        _even_inproj_kernel,
        grid=(m // tm,),
        in_specs=[
            pl.BlockSpec((tm, D_MODEL), lambda i: (i, 0)),
            pl.BlockSpec((1, D_MODEL), lambda i: (0, 0)),
            pl.BlockSpec((D_MODEL, n_out), lambda i: (0, 0), pipeline_mode=pl.Buffered(1)),
            pl.BlockSpec((D_MODEL, LANES), lambda i: (0, 0), pipeline_mode=pl.Buffered(1)),
        ],
        out_specs=[
            pl.BlockSpec((tm, n_out), lambda i: (i, 0)),
            pl.BlockSpec((tm, LANES), lambda i: (i, 0)),
        ],
        out_shape=[
            jax.ShapeDtypeStruct((m, n_out), BF16),
            jax.ShapeDtypeStruct((m, LANES), F32),
        ],
        compiler_params=pltpu.CompilerParams(
            dimension_semantics=("arbitrary",), vmem_limit_bytes=VMEM_LIMIT_BYTES),
        name="even_inproj",
    )(x2, nw, w, wdt)


def _even_mix_kernel(xbc_ref, za_ref, gv_ref, gg_ref, zb_ref, dt_ref,
                     cw_ref, cb_ref, dtb_ref, alog_ref, dskip_ref, nrm_ref,
                     dww_ref, dwb_ref, lnw_ref, lnb_ref,
                     e64_ref, e128_ref, tri_ref, blk_ref,
                     out_ref,
                     xbuf, ubuf, hs, ybuf, uph, cbuf):
    t = pl.program_id(1)
    rows = za_ref.shape[0]
    d_ssd = SSD_HEADS * SSD_HEADDIM
    gw = HEADS_PER_GROUP * SSD_HEADDIM

    @pl.when(t == 0)
    def _():
        xbuf[0:XCARRY, :] = jnp.zeros((XCARRY, xbuf.shape[1]), F32)
        ubuf[0:UCARRY, :] = jnp.zeros((UCARRY, ubuf.shape[1]), F32)
        hs[...] = jnp.zeros(hs.shape, F32)

    xbuf[XCARRY:XCARRY + rows, :] = xbc_ref[...].astype(F32)
    acc = cb_ref[...]
    for k in range(SSD_CONV):
        off = XCARRY - (SSD_CONV - 1) + k
        acc = acc + cw_ref[k:k + 1, :] * xbuf[off:off + rows, :]
    xbuf[0:XCARRY, :] = xbuf[rows:rows + XCARRY, :]
    xc = acc * jax.nn.sigmoid(acc)
    xs = xc[:, :d_ssd]
    bm = xc[:, d_ssd:d_ssd + SSD_GROUPS * D_STATE]
    cm = xc[:, d_ssd + SSD_GROUPS * D_STATE:]

    dtr = dt_ref[...] + dtb_ref[...]
    dt = jnp.maximum(dtr, 0.0) + jnp.log1p(jnp.exp(-jnp.abs(dtr)))
    da = dt * (-jnp.exp(alog_ref[...]))
    cs = _left_apply(tri_ref, da)
    tot = _left_apply(blk_ref, da)
    dt_x = _expand(dt, e64_ref)
    ecs_x = _expand(jnp.exp(cs), e64_ref)
    edst_x = _expand(jnp.exp(tot - cs), e64_ref)
    ecd_x = _expand(jnp.exp(tot), e64_ref)
    cs_col = _expand(cs, e128_ref)
    cs_t = cs.T

    xd = xs * dt_x
    xdd = xd * edst_x

    li = lax.broadcasted_iota(jnp.int32, (PAIR, PAIR), 0)
    si = lax.broadcasted_iota(jnp.int32, (PAIR, PAIR), 1)
    intra = (si <= li) & ((si // CHUNK) == (li // CHUNK))
    lane_g = lax.broadcasted_iota(jnp.int32, (PAIR, gw), 1)
    row_g = lax.broadcasted_iota(jnp.int32, (PAIR, gw), 0)

    for pr in range(rows // PAIR):
        r0 = pr * PAIR
        for g in range(SSD_GROUPS):
            c_g = cm[r0:r0 + PAIR, g * D_STATE:(g + 1) * D_STATE].astype(BF16)
            b_g = bm[r0:r0 + PAIR, g * D_STATE:(g + 1) * D_STATE]
            cb = _dot_nt(c_g, b_g.astype(BF16))
            b_gt = b_g.T.astype(BF16)
            xd_g = xd[r0:r0 + PAIR, g * gw:(g + 1) * gw]
            xdd_g = xdd[r0:r0 + PAIR, g * gw:(g + 1) * gw]
            m_parts = []
            w_parts = []
            for e in range(HEADS_PER_GROUP):
                h = g * HEADS_PER_GROUP + e
                seg = cs_col[r0:r0 + PAIR, h * LANES:(h + 1) * LANES] - cs_t[h:h + 1, r0:r0 + PAIR]
                decay = jnp.exp(jnp.where(intra, seg, NEG))
                m_parts.append((cb * decay).astype(BF16))
                in_head = (lane_g // SSD_HEADDIM) == e
                w_parts.append(jnp.where(in_head, xd_g, 0.0).astype(BF16))
            y_diag = _dot(jnp.concatenate(m_parts, axis=1), jnp.concatenate(w_parts, axis=0))
            for j in range(PAIR // CHUNK):
                rj = r0 + j * CHUNK
                h_prev = hs[g]
                y_off = _dot(c_g[j * CHUNK:(j + 1) * CHUNK], h_prev.astype(BF16))
                y_off = y_off * ecs_x[rj:rj + CHUNK, g * gw:(g + 1) * gw]
                ybuf[rj:rj + CHUNK, g * gw:(g + 1) * gw] = y_diag[j * CHUNK:(j + 1) * CHUNK] + y_off
                in_chunk = (row_g // CHUNK) == j
                st = _dot(b_gt, jnp.where(in_chunk, xdd_g, 0.0).astype(BF16))
                hs[g] = h_prev * ecd_x[rj:rj + 1, g * gw:(g + 1) * gw] + st

    y = ybuf[...] + xs * dskip_ref[...]
    za = za_ref[...].astype(F32)
    y = y * (za * jax.nn.sigmoid(za))
    for g in range(SSD_GROUPS):
        yg = y[:, g * gw:(g + 1) * gw]
        out_ref[:, g * gw:(g + 1) * gw] = _rms_rows(yg, nrm_ref[:, g * gw:(g + 1) * gw]).astype(BF16)

    gg = gg_ref[...].astype(F32)
    ubuf[UCARRY:UCARRY + rows, :] = gv_ref[...].astype(F32) * jax.nn.sigmoid(gg)
    base = UCARRY - (CONF_KERNEL - 1)
    ntile = D_MODEL // LANES
    for sh in range(SUBLANES):
        span = UCARRY + rows - (SUBLANES if sh else 0)
        for c in range(ntile):
            uph[sh, c, 0:span, :] = ubuf[sh:sh + span, c * LANES:(c + 1) * LANES]
    ubuf[0:UCARRY, :] = ubuf[rows:rows + UCARRY, :]
    nblk = rows // CONV_ROWS

    def conv_block(i, carry):
        c = i // nblk
        r0 = pl.multiple_of((i % nblk) * CONV_ROWS, CONV_ROWS)
        nsub = CONV_ROWS // SUBLANES
        a = jnp.broadcast_to(dwb_ref[c], (nsub, SUBLANES, LANES))
        for k in range(CONF_KERNEL):
            sh = (base + k) % SUBLANES
            start = pl.multiple_of(r0 + (base + k - sh), SUBLANES)
            blk = uph[sh, c, pl.ds(start, CONV_ROWS), :].reshape(nsub, SUBLANES, LANES)
            a = a + dww_ref[k, c] * blk
        cbuf[c, pl.ds(r0, CONV_ROWS), :] = a.reshape(CONV_ROWS, LANES)
        return carry

    lax.fori_loop(0, ntile * nblk, conv_block, 0)
    acc_u = jnp.concatenate([cbuf[c] for c in range(ntile)], axis=1)
    mu = jnp.mean(acc_u, axis=-1, keepdims=True)
    cen = acc_u - mu
    var = jnp.mean(cen * cen, axis=-1, keepdims=True)
    un = cen * lax.rsqrt(var + LN_EPS) * lnw_ref[...] + lnb_ref[...]
    un = un * jax.nn.sigmoid(un)
    zb = zb_ref[...].astype(F32)
    out_ref[:, d_ssd:] = (un * (zb * jax.nn.sigmoid(zb))).astype(BF16)


def _even_mix(proj, dt_raw, p, bsz, seq):
    rows = MIX_ROWS
    nt = seq // rows
    d_ssd = SSD_HEADS * SSD_HEADDIM
    xbc_dim = d_ssd + 2 * SSD_GROUPS * D_STATE
    m = bsz * seq

    def rowblk(col):
        return lambda b, t: (b * nt + t, col)

    def const(shape):
        return pl.BlockSpec(shape, lambda b, t: (0, 0))

    in_specs = [
        pl.BlockSpec((rows, xbc_dim), rowblk(0)),
        pl.BlockSpec((rows, D_MODEL), rowblk(2)),
        pl.BlockSpec((rows, D_MODEL), rowblk(3)),
        pl.BlockSpec((rows, D_MODEL), rowblk(4)),
        pl.BlockSpec((rows, D_MODEL), rowblk(5)),
        pl.BlockSpec((rows, LANES), rowblk(0)),
        const((SSD_CONV, xbc_dim)), const((1, xbc_dim)),
        const((1, LANES)), const((1, LANES)),
        const((1, d_ssd)), const((1, d_ssd)),
        pl.BlockSpec((CONF_KERNEL, D_MODEL // LANES, SUBLANES, LANES), lambda b, t: (0, 0, 0, 0)),
        pl.BlockSpec((D_MODEL // LANES, SUBLANES, LANES), lambda b, t: (0, 0, 0)),
        const((1, D_MODEL)), const((1, D_MODEL)),
        const((2 * LANES, d_ssd)), const((2 * LANES, SSD_HEADS * LANES)),
        const((rows, rows)), const((rows, rows)),
    ]
    return pl.pallas_call(
        _even_mix_kernel,
        grid=(bsz, nt),
        in_specs=in_specs,
        out_specs=pl.BlockSpec((rows, 2 * D_MODEL), rowblk(0)),
        out_shape=jax.ShapeDtypeStruct((m, 2 * D_MODEL), BF16),
        scratch_shapes=[
            pltpu.VMEM((XCARRY + rows, xbc_dim), F32),
            pltpu.VMEM((UCARRY + rows, D_MODEL), F32),
            pltpu.VMEM((SSD_GROUPS, D_STATE, HEADS_PER_GROUP * SSD_HEADDIM), F32),
            pltpu.VMEM((rows, d_ssd), F32),
            pltpu.VMEM((SUBLANES, D_MODEL // LANES, UCARRY + rows, LANES), F32),
            pltpu.VMEM((D_MODEL // LANES, rows, LANES), F32),
        ],
        compiler_params=pltpu.CompilerParams(
            dimension_semantics=("arbitrary", "arbitrary"), vmem_limit_bytes=VMEM_LIMIT_BYTES),
        name="even_mix",
    )(proj, proj, proj, proj, proj, dt_raw,
      p["conv_w"], p["conv_b"], p["dt_bias"], p["a_log"], p["d_skip"], p["ssd_norm_w"],
      p["dw_w"], p["dw_b"], p["ln_w"], p["ln_b"],
      p["e64"], p["e128"], p["tri"], p["blk"])


def _outproj_kernel(y_ref, w_ref, x_ref, o_ref):
    o_ref[...] = x_ref[...] + _dot(y_ref[...], w_ref[...])


def _outproj(y, w, x2, name):
    m, kdim = y.shape
    tm = PROJ_ROWS
    return pl.pallas_call(
        _outproj_kernel,
        grid=(m // tm,),
        in_specs=[
            pl.BlockSpec((tm, kdim), lambda i: (i, 0)),
            pl.BlockSpec((kdim, D_MODEL), lambda i: (0, 0), pipeline_mode=pl.Buffered(1)),
            pl.BlockSpec((tm, D_MODEL), lambda i: (i, 0)),
        ],
        out_specs=pl.BlockSpec((tm, D_MODEL), lambda i: (i, 0)),
        out_shape=jax.ShapeDtypeStruct((m, D_MODEL), F32),
        compiler_params=pltpu.CompilerParams(
            dimension_semantics=("arbitrary",), vmem_limit_bytes=VMEM_LIMIT_BYTES),
        name=name,
    )(y, w, x2)


def _odd_inproj_kernel(x_ref, pos_ref, nw_ref, w_ref, wvt_ref, qnw_ref, knw_ref, invf_ref, sgn_ref,
                       sel_ref, one_ref, gmean_ref, q_ref, k_ref, vt_ref, g_ref):
    hn = _rms_rows(x_ref[...], nw_ref[...]).astype(BF16)
    d = D_MODEL
    half = ROT_DIM // 2
    ang_t = invf_ref[...] * pos_ref[0].astype(F32)
    pad = jnp.zeros((LANES - half, ang_t.shape[1]), F32)
    cos_c = jnp.concatenate([jnp.cos(ang_t), pad], axis=0).T
    sin_c = jnp.concatenate([jnp.sin(ang_t), pad], axis=0).T
    reps = d // LANES
    cos = jnp.concatenate([_expand(cos_c, sel_ref) + one_ref[...]] * reps, axis=1)
    sin = jnp.concatenate([_expand(sin_c, sel_ref) * sgn_ref[...]] * reps, axis=1)
    lane = lax.broadcasted_iota(jnp.int32, (1, d), 1) % DIFF_DK
    first_half = lane < (ROT_DIM // 2)
    gblk = gmean_ref.shape[0]

    def norm_rot(z, w_row, scale):
        sq = (z * z).astype(BF16)
        parts = []
        for c in range(0, d, gblk):
            parts.append(_dot(sq[:, c:c + gblk], gmean_ref[...]))
        ms = jnp.concatenate(parts, axis=1)
        zn = z * lax.rsqrt(ms + EPS) * w_row
        swapped = jnp.where(first_half, pltpu.roll(zn, d - ROT_DIM // 2, 1), pltpu.roll(zn, ROT_DIM // 2, 1))
        out = zn * cos + swapped * sin
        if scale != 1.0:
            out = out * scale
        return out.astype(BF16)

    q = _dot(hn, w_ref[:, 0:d])
    k = _dot(hn, w_ref[:, d:2 * d])
    q_ref[...] = norm_rot(q, qnw_ref[...], math.log2(math.e) * DIFF_DK ** -0.5)
    vt_ref[0] = _dot_nt(wvt_ref[...], hn).astype(BF16)
    k_ref[...] = norm_rot(k, knw_ref[...], 1.0)
    g_ref[...] = _dot(hn, w_ref[:, 2 * d:3 * d]).astype(BF16)


def _odd_inproj(x2, pos2, p, bsz, seq):
    m = x2.shape[0]
    tm = PROJ_ROWS
    nt = seq // tm
    row = lambda b, i: (b * nt + i, 0)
    const = lambda b, i: (0, 0)
    gblk = p["gmean"].shape[0]
    out_sds = jax.ShapeDtypeStruct((m, D_MODEL), BF16)
    return pl.pallas_call(
        _odd_inproj_kernel,
        grid=(bsz, nt),
        in_specs=[
            pl.BlockSpec((tm, D_MODEL), row),
            pl.BlockSpec((1, 1, tm), lambda b, i: (b * nt + i, 0, 0)),
            pl.BlockSpec((1, D_MODEL), const),
            pl.BlockSpec((D_MODEL, 3 * D_MODEL), const, pipeline_mode=pl.Buffered(1)),
            pl.BlockSpec((D_MODEL, D_MODEL), const, pipeline_mode=pl.Buffered(1)),
            pl.BlockSpec((1, D_MODEL), const),
            pl.BlockSpec((1, D_MODEL), const),
            pl.BlockSpec((ROT_DIM // 2, 1), const),
            pl.BlockSpec((1, LANES), const),
            pl.BlockSpec((2 * LANES, LANES), const),
            pl.BlockSpec((1, LANES), const),
            pl.BlockSpec((gblk, gblk), const),
        ],
        out_specs=[pl.BlockSpec((tm, D_MODEL), row), pl.BlockSpec((tm, D_MODEL), row),
                   pl.BlockSpec((1, D_MODEL, tm), lambda b, i: (b, 0, i)),
                   pl.BlockSpec((tm, D_MODEL), row)],
        out_shape=[out_sds, out_sds, jax.ShapeDtypeStruct((bsz, D_MODEL, seq), BF16), out_sds],
        compiler_params=pltpu.CompilerParams(
            dimension_semantics=("arbitrary", "arbitrary"), vmem_limit_bytes=VMEM_LIMIT_BYTES),
        name="odd_inproj",
    )(x2, pos2.reshape(m // tm, 1, tm), p["norm_w"], p["w_qkg"], p["w_vt"], p["q_norm_w"], p["k_norm_w"],
      p["invf"], p["sgn"], p["sel"], p["one"], p["gmean"])


def _attn_kernel(q_ref, k_ref, vt_ref, gate_ref, lq1_ref, lk1_ref, lq2_ref, lk2_ref, sub_ref,
                 o_ref, acc_sc, s_a, s_b, tmax_sc, ml_sc):
    qi = pl.program_id(2)
    tq = q_ref.shape[1]
    tk = ATT_TK
    assert tq == tk
    q = q_ref[0]
    lane = lax.broadcasted_iota(jnp.int32, q.shape, 1)
    zero = jnp.zeros_like(q)
    q_maps = (jnp.where(lane < DIFF_DK, q, zero), jnp.where(lane >= DIFF_DK, q, zero))

    acc_sc[...] = jnp.zeros(acc_sc.shape, F32)
    ml_sc[0] = jnp.full(ml_sc.shape[1:], NEG, F32)
    ml_sc[1] = jnp.zeros(ml_sc.shape[1:], F32)

    def tile_start(u):
        return pl.multiple_of(jnp.where(u == 0, qi, u - 1) * tk, tk)

    hq = tq // 2

    def scores_part(u, s_buf, slot, masked, mp, c):
        kj = k_ref[0, pl.ds(tile_start(u), tk), :]
        s = _dot_nt(kj, q_maps[mp][c:c + hq])
        if masked:
            key = lax.broadcasted_iota(jnp.int32, (tk, hq), 0)
            qry = lax.broadcasted_iota(jnp.int32, (tk, hq), 1) + c
            s = jnp.where((key // CHUNK) <= (qry // CHUNK), s, NEG)
        s_buf[mp, :, c:c + hq] = s
        tmax_sc[slot, mp, :, c:c + hq] = jnp.max(s, axis=0, keepdims=True)

    def scores(u, s_buf, slot, masked):
        for mp in range(2):
            for c in (0, hq):
                scores_part(u, s_buf, slot, masked, mp, c)

    def absorb_part(u, s_buf, slot, mp, c):
        vtj = vt_ref[0, :, pl.ds(tile_start(u), tk)]
        m_prev = ml_sc[0, mp, :, c:c + hq]
        m_new = jnp.maximum(m_prev, tmax_sc[slot, mp, :, c:c + hq])
        p = jnp.exp2(s_buf[mp, :, c:c + hq] - m_new)
        alpha = jnp.exp2(m_prev - m_new)
        ml_sc[1, mp, :, c:c + hq] = alpha * ml_sc[1, mp, :, c:c + hq] + jnp.sum(p, axis=0, keepdims=True)
        acc_sc[mp, :, c:c + hq] = alpha * acc_sc[mp, :, c:c + hq] + _dot(vtj, p.astype(BF16))
        ml_sc[0, mp, :, c:c + hq] = m_new

    def absorb(u, s_buf, slot):
        for mp in range(2):
            for c in (0, hq):
                absorb_part(u, s_buf, slot, mp, c)

    scores(0, s_a, 0, True)

    bufs = ((s_a, 0), (s_b, 1))
    assert ATT_UNROLL % 2 == 0

    def steps(u0, count, tail):
        for j in range(count):
            nxt, cur = bufs[(j + 1) % 2], bufs[j % 2]
            for mp in range(2):
                for c in (0, hq):
                    scores_part(u0 + j + 1, nxt[0], nxt[1], False, mp, c)
                    absorb_part(u0 + j, cur[0], cur[1], mp, c)
        if tail:
            last = bufs[count % 2]
            absorb(u0 + count, last[0], last[1])

    def body(t, carry):
        steps(ATT_UNROLL * t, ATT_UNROLL, False)
        return carry

    lax.fori_loop(0, qi // ATT_UNROLL, body, 0)
    done = (qi // ATT_UNROLL) * ATT_UNROLL
    for r in range(ATT_UNROLL):
        @pl.when(qi - done == r)
        def _(r=r):
            steps(done, r, True)

    lam = (jnp.exp(jnp.sum(lq1_ref[...] * lk1_ref[...], axis=-1, keepdims=True))
           - jnp.exp(jnp.sum(lq2_ref[...] * lk2_ref[...], axis=-1, keepdims=True)) + LAM_INIT)
    o_t = acc_sc[0] / ml_sc[1, 0] - lam * (acc_sc[1] / ml_sc[1, 1])
    o = _rms_rows(o_t.T, sub_ref[...]) * (1.0 - LAM_INIT)
    g = gate_ref[0].astype(F32)
    o_ref[0] = (o * (g * jax.nn.sigmoid(g))).astype(BF16)


def _attention(q, k, vt, gate, p):
    bsz, seq, _ = q.shape
    tq = ATT_TQ
    qblk = pl.BlockSpec((1, tq, DIFF_DV), lambda b, h, i: (b, i, h))
    kblk = pl.BlockSpec((1, seq, DIFF_DV), lambda b, h, i: (b, 0, h))
    vtblk = pl.BlockSpec((1, DIFF_DV, seq), lambda b, h, i: (b, h, 0))
    vec = pl.BlockSpec((1, DIFF_DK), lambda b, h, i: (0, 0))
    return pl.pallas_call(
        _attn_kernel,
        grid=(bsz, DIFF_HEADS, seq // tq),
        in_specs=[qblk, kblk, vtblk, qblk, vec, vec, vec, vec,
                  pl.BlockSpec((1, DIFF_DV), lambda b, h, i: (0, 0))],
        out_specs=qblk,
        out_shape=jax.ShapeDtypeStruct((bsz, seq, DIFF_HEADS * DIFF_DV), BF16),
        scratch_shapes=[
            pltpu.VMEM((2, DIFF_DV, tq), F32),
            pltpu.VMEM((2, ATT_TK, tq), F32),
            pltpu.VMEM((2, ATT_TK, tq), F32),
            pltpu.VMEM((2, 2, 1, tq), F32),
            pltpu.VMEM((2, 2, 1, tq), F32),
        ],
        compiler_params=pltpu.CompilerParams(
            dimension_semantics=("arbitrary", "arbitrary", "arbitrary"),
            vmem_limit_bytes=VMEM_LIMIT_BYTES),
        name="diff_attn",
    )(q, k, vt, gate, p["lq1"], p["lk1"], p["lq2"], p["lk2"], p["subln_w"])


def _even_params(norm_w, w_in, conv_w, conv_b, dt_bias, a_log, d_skip, ssd_norm_w,
                 dw_w, dw_b, ln_w, ln_b, w_out):
    d_ssd = SSD_HEADS * SSD_HEADDIM
    xbc_dim = d_ssd + 2 * SSD_GROUPS * D_STATE
    o_za, o_xbc, o_dt = 0, d_ssd, d_ssd + xbc_dim
    o_gv = o_dt + SSD_HEADS
    o_gg, o_zb = o_gv + D_MODEL, o_gv + 2 * D_MODEL
    w_main = jnp.concatenate([
        w_in[:, o_xbc:o_xbc + xbc_dim], w_in[:, o_za:o_za + d_ssd],
        w_in[:, o_gv:o_gv + D_MODEL], w_in[:, o_gg:o_gg + D_MODEL], w_in[:, o_zb:o_zb + D_MODEL],
    ], axis=1).astype(BF16)
    pad = LANES - SSD_HEADS
    w_dt = jnp.pad(w_in[:, o_dt:o_dt + SSD_HEADS], ((0, 0), (0, pad))).astype(BF16)

    heads = np.arange(LANES)[:, None]
    e64 = (heads == (np.arange(d_ssd)[None, :] // SSD_HEADDIM)).astype(np.float32)
    e128 = (heads == (np.arange(SSD_HEADS * LANES)[None, :] // LANES)).astype(np.float32)
    r = np.arange(MIX_ROWS)
    same = (r[:, None] // CHUNK) == (r[None, :] // CHUNK)
    tri = (same & (r[None, :] <= r[:, None])).astype(np.float32)
    return dict(
        norm_w=norm_w[None, :], w_main=w_main, w_dt=w_dt,
        conv_w=conv_w, conv_b=conv_b[None, :],
        dt_bias=jnp.pad(dt_bias, (0, pad))[None, :], a_log=jnp.pad(a_log, (0, pad))[None, :],
        d_skip=jnp.repeat(d_skip, SSD_HEADDIM)[None, :], ssd_norm_w=ssd_norm_w[None, :],
        dw_w=jnp.broadcast_to(dw_w.reshape(CONF_KERNEL, D_MODEL // LANES, 1, LANES),
                              (CONF_KERNEL, D_MODEL // LANES, SUBLANES, LANES)),
        dw_b=jnp.broadcast_to(dw_b.reshape(D_MODEL // LANES, 1, LANES), (D_MODEL // LANES, SUBLANES, LANES)),
        ln_w=ln_w[None, :], ln_b=ln_b[None, :],
        e64=jnp.asarray(np.concatenate([e64, e64]), BF16), e128=jnp.asarray(np.concatenate([e128, e128]), BF16),
        tri=jnp.asarray(tri, BF16), blk=jnp.asarray(same.astype(np.float32), BF16),
        w_out=w_out.astype(BF16),
    )


def _odd_params(norm_w, w_in, q_norm_w, k_norm_w, lq1, lk1, lq2, lk2, subln_w, w_out):
    maps = D_MODEL // DIFF_DK
    lane = np.arange(LANES) % DIFF_DK
    half = ROT_DIM // 2
    inv = (ROPE_THETA ** (-2.0 * jnp.arange(half, dtype=F32) / ROT_DIM))
    rot = lane < ROT_DIM
    sgn = np.where(lane < half, -1.0, 1.0).astype(np.float32)[None, :]
    sel = ((np.arange(LANES)[:, None] == (lane % half)[None, :]) & rot[None, :]).astype(np.float32)
    one = (~rot).astype(np.float32)[None, :]
    gblk = 2 * LANES
    gi = np.arange(gblk) // DIFF_DK
    gmean = (gi[:, None] == gi[None, :]).astype(np.float32) / DIFF_DK
    return dict(
        norm_w=norm_w[None, :],
        w_qkg=jnp.concatenate([w_in[:, :2 * D_MODEL], w_in[:, 3 * D_MODEL:]], axis=1).astype(BF16),
        w_vt=w_in[:, 2 * D_MODEL:3 * D_MODEL].T.astype(BF16),
        q_norm_w=jnp.tile(q_norm_w, maps)[None, :], k_norm_w=jnp.tile(k_norm_w, maps)[None, :],
        invf=inv.astype(F32)[:, None], sgn=jnp.asarray(sgn),
        sel=jnp.asarray(np.concatenate([sel, sel]), BF16), one=jnp.asarray(one), gmean=jnp.asarray(gmean, BF16),
        lq1=lq1[None, :], lk1=lk1[None, :], lq2=lq2[None, :], lk2=lk2[None, :],
        subln_w=subln_w[None, :], w_out=w_out.astype(BF16),
    )


def kernel(x, positions, a_norm_w, a_w_in, a_conv_w, a_conv_b, a_dt_bias, a_a_log, a_d_skip,
           a_ssd_norm_w, a_dw_w, a_dw_b, a_ln_w, a_ln_b, a_w_out, c_norm_w, c_w_in, c_q_norm_w,
           c_k_norm_w, c_lq1, c_lk1, c_lq2, c_lk2, c_subln_w, c_w_out):
    bsz, seq, d = x.shape
    assert d == D_MODEL and seq % max(MIX_ROWS, ATT_TQ, PROJ_ROWS) == 0
    assert a_norm_w.shape[0] == 1 and c_norm_w.shape[0] == 1
    m = bsz * seq
    x2 = x.reshape(m, d)

    pe = _even_params(a_norm_w[0], a_w_in[0], a_conv_w[0], a_conv_b[0], a_dt_bias[0], a_a_log[0],
                      a_d_skip[0], a_ssd_norm_w[0], a_dw_w[0], a_dw_b[0], a_ln_w[0], a_ln_b[0], a_w_out[0])
    proj, dt_raw = _even_inproj(x2, pe["norm_w"], pe["w_main"], pe["w_dt"])
    y = _even_mix(proj, dt_raw, pe, bsz, seq)
    x2 = _outproj(y, pe["w_out"], x2, "even_outproj")

    po = _odd_params(c_norm_w[0], c_w_in[0], c_q_norm_w[0], c_k_norm_w[0], c_lq1[0], c_lk1[0],
                     c_lq2[0], c_lk2[0], c_subln_w[0], c_w_out[0])
    q, k, vt, gate = _odd_inproj(x2, positions.reshape(m, 1), po, bsz, seq)
    shp = (bsz, seq, d)
    o = _attention(q.reshape(shp), k.reshape(shp), vt, gate.reshape(shp), po)
    x2 = _outproj(o.reshape(m, d), po["w_out"], x2, "odd_outproj")
    return x2.reshape(bsz, seq, d)
```
